```python
import jax, jax.numpy as jnp
from jax import lax
import numpy as np

D_MODEL = 1024
BATCH = 4
SEQ = 8192
DEPTH = 2

D_MIX = D_MODEL
LRU_WIDTH = D_MIX // 2
LRU_BLOCKS = 8
LRU_BLOCK = LRU_WIDTH // LRU_BLOCKS
LRU_CONV = 4
LRU_C = 8.0
RET_HEADS = 4
RET_WIDTH = D_MIX // 4
RET_HEAD_DIM = RET_WIDTH // RET_HEADS
RET_CHUNK = 128
ROPE_BASE = 10000.0
HG_HEADS = 4
HG_WIDTH = D_MIX // 4
HG_HEAD_DIM = HG_WIDTH // HG_HEADS
HG_CHUNK = 64
IN_SPLITS = [LRU_WIDTH, LRU_WIDTH] + [RET_WIDTH] * 4 + [HG_WIDTH] * 4
IN_COLS = sum(IN_SPLITS)
D_FF = ((8 * D_MODEL // 3 + 255) // 256) * 256
FFN_CONV = 3
NORM_EPS = 1e-6

kernel_name = "hymba_style_rglru_retention_hgrn2_convffn"


def rmsnorm(x, w):
    xf = x.astype(jnp.float32)
    y = xf * lax.rsqrt(jnp.mean(xf * xf, axis=-1, keepdims=True) + NORM_EPS)
    return (y * w.astype(jnp.float32)).astype(x.dtype)


def head_rmsnorm(o, w):
    B, S, H, d = o.shape
    y = o * lax.rsqrt(jnp.mean(o * o, axis=-1, keepdims=True) + NORM_EPS)
    return y.reshape(B, S, H * d) * w.astype(jnp.float32)


def causal_dwconv(x, w, b):
    K = w.shape[0]
    S = x.shape[1]
    xp = jnp.pad(x, ((0, 0), (K - 1, 0), (0, 0)))
    y = b + xp[:, 0:S] * w[0]
    for k in range(1, K):
        y = y + xp[:, k:k + S] * w[k]
    return y


def rotary(x):
    S, d = x.shape[1], x.shape[-1]
    inv = ROPE_BASE ** (-jnp.arange(0, d, 2, dtype=jnp.float32) / d)
    ang = jnp.arange(S, dtype=jnp.float32)[:, None] * inv[None, :]
    cos = jnp.cos(ang)[None, :, None, :]
    sin = jnp.sin(ang)[None, :, None, :]
    x1, x2 = x[..., : d // 2], x[..., d // 2:]
    return jnp.concatenate([x1 * cos - x2 * sin, x2 * cos + x1 * sin], axis=-1)


def to_chunks(x, C):
    B, S, H, d = x.shape
    return x.reshape(B, S // C, C, H, d).transpose(1, 0, 3, 2, 4)


def from_chunks(x):
    NC, B, H, C, d = x.shape
    return x.transpose(1, 0, 3, 2, 4).reshape(B, NC * C, H, d)


def rglru_group(xr, gr, conv_w, conv_b, wa, ba, wx, bx, lam):
    dt = xr.dtype
    xc = causal_dwconv(xr, conv_w, conv_b).astype(jnp.float32)
    B, S, _ = xc.shape
    xb = xc.reshape(B, S, LRU_BLOCKS, LRU_BLOCK)
    r = jax.nn.sigmoid(jnp.einsum('bshi,hij->bshj', xb, wa.astype(jnp.float32)).reshape(B, S, LRU_WIDTH) + ba.astype(jnp.float32))
    i = jax.nn.sigmoid(jnp.einsum('bshi,hij->bshj', xb, wx.astype(jnp.float32)).reshape(B, S, LRU_WIDTH) + bx.astype(jnp.float32))
    log_a = -LRU_C * r * jax.nn.softplus(-lam.astype(jnp.float32))
    a = jnp.exp(log_a)
    u = jnp.sqrt(-jnp.expm1(2.0 * log_a)) * (i * xc)

    def combine(c1, c2):
        a1, b1 = c1
        a2, b2 = c2
        return a1 * a2, a2 * b1 + b2

    _, h = lax.associative_scan(combine, (a, u), axis=1)
    return (h * jax.nn.gelu(gr.astype(jnp.float32))).astype(dt)


def retention_group(q, k, v, g, norm_w):
    dt = q.dtype
    B, S, _ = q.shape
    H, dh, C = RET_HEADS, RET_HEAD_DIM, RET_CHUNK
    q = rotary(q.astype(jnp.float32).reshape(B, S, H, dh))
    k = rotary(k.astype(jnp.float32).reshape(B, S, H, dh)) * (dh ** -0.5)
    v = v.astype(jnp.float32).reshape(B, S, H, dh)
    log_gamma = jnp.log1p(-jnp.exp2(-5.0 - jnp.arange(H, dtype=jnp.float32)))
    idx = jnp.arange(C, dtype=jnp.float32)
    rel = idx[:, None] - idx[None, :]
    intra = jnp.where(rel >= 0, jnp.exp(jnp.maximum(rel, 0.0) * log_gamma[:, None, None]), 0.0)
    q_dec = jnp.exp((idx + 1.0)[None, :] * log_gamma[:, None])[..., None]
    k_dec = jnp.exp((C - 1.0 - idx)[None, :] * log_gamma[:, None])[..., None]
    chunk_dec = jnp.exp(C * log_gamma)[:, None, None]

    def step(state, inp):
        qc, kc, vc = inp
        scores = jnp.einsum('bhnd,bhmd->bhnm', qc, kc) * intra
        o = jnp.einsum('bhnm,bhme->bhne', scores, vc) + jnp.einsum('bhnd,bhde->bhne', qc, state) * q_dec
        state = state * chunk_dec + jnp.einsum('bhmd,bhme->bhde', kc * k_dec, vc)
        return state, o

    state0 = jnp.zeros((B, H, dh, dh), jnp.float32)
    _, o = lax.scan(step, state0, (to_chunks(q, C), to_chunks(k, C), to_chunks(v, C)))
    y = head_rmsnorm(from_chunks(o), norm_w)
    return (y * jax.nn.silu(g.astype(jnp.float32))).astype(dt)


def hgrn2_group(q, fpre, i, g, lb, norm_w):
    dt = q.dtype
    B, S, _ = q.shape
    H, dh, C = HG_HEADS, HG_HEAD_DIM, HG_CHUNK
    q = jax.nn.silu(q.astype(jnp.float32)).reshape(B, S, H, dh)
    fp = fpre.astype(jnp.float32).reshape(B, S, H, dh)
    lb = lb.reshape(H, dh)
    log_f = jnp.logaddexp(jnp.log(lb), jnp.log1p(-lb) + jax.nn.log_sigmoid(fp))
    k = (1.0 - lb) * jax.nn.sigmoid(-fp)
    v = i.astype(jnp.float32).reshape(B, S, H, dh)
    mask = jnp.tril(jnp.ones((C, C), dtype=bool))[None, None, :, :, None]

    def step(state, inp):
        qc, kc, gc, vc = inp
        b = jnp.cumsum(gc, axis=2)
        diff = b[:, :, :, None, :] - b[:, :, None, :, :]
        decay = jnp.exp(jnp.where(mask, diff, -jnp.inf))
        scores = jnp.einsum('bhnd,bhmd,bhnmd->bhnm', qc, kc, decay)
        o = jnp.einsum('bhnm,bhme->bhne', scores, vc) + jnp.einsum('bhnd,bhde->bhne', qc * jnp.exp(b), state)
        b_last = b[:, :, -1:, :]
        state = state * jnp.exp(b_last[:, :, 0, :, None]) + jnp.einsum('bhmd,bhme->bhde', kc * jnp.exp(b_last - b), vc)
        return state, o

    state0 = jnp.zeros((B, H, dh, dh), jnp.float32)
    _, o = lax.scan(step, state0, (to_chunks(q, C), to_chunks(k, C), to_chunks(log_f, C), to_chunks(v, C)))
    y = head_rmsnorm(from_chunks(o), norm_w)
    return (y * jax.nn.silu(g.astype(jnp.float32))).astype(dt)


def conv_ffn(h, w_up, conv_w, conv_b, w_down):
    u = causal_dwconv(h @ w_up, conv_w, conv_b)
    gate, val = jnp.split(u, 2, axis=-1)
    return (jax.nn.silu(gate) * val) @ w_down


def setup_inputs(seed: int = 0) -> dict:
    key = jax.random.key(seed)
    ks = jax.random.split(key, 24)
    f32 = jnp.float32
    nrm = lambda k, shape, s: jax.random.normal(k, shape, f32) * s
    u = jax.random.uniform(ks[9], (DEPTH, LRU_WIDTH), f32, 0.9, 0.999)
    a0 = u ** (1.0 / LRU_C)
    lam = jnp.log(a0) - jnp.log1p(-a0)
    return {
        "x": nrm(ks[0], (BATCH, SEQ, D_MODEL), 1.0),
        "norm1_w": 1.0 + nrm(ks[1], (DEPTH, D_MODEL), 0.02),
        "w_in": nrm(ks[2], (DEPTH, D_MODEL, IN_COLS), D_MODEL ** -0.5),
        "lru_conv_w": nrm(ks[3], (DEPTH, LRU_CONV, LRU_WIDTH), LRU_CONV ** -0.5),
        "lru_conv_b": nrm(ks[4], (DEPTH, LRU_WIDTH), 0.01),
        "lru_wa": nrm(ks[5], (DEPTH, LRU_BLOCKS, LRU_BLOCK, LRU_BLOCK), LRU_BLOCK ** -0.5),
        "lru_ba": nrm(ks[6], (DEPTH, LRU_WIDTH), 0.01),
        "lru_wx": nrm(ks[7], (DEPTH, LRU_BLOCKS, LRU_BLOCK, LRU_BLOCK), LRU_BLOCK ** -0.5),
        "lru_bx": nrm(ks[8], (DEPTH, LRU_WIDTH), 0.01),
        "lru_lambda": lam,
        "ret_norm_w": 1.0 + nrm(ks[10], (DEPTH, RET_WIDTH), 0.02),
        "hg_lower_bounds": nrm(ks[11], (DEPTH, HG_WIDTH), 0.5),
        "hg_norm_w": 1.0 + nrm(ks[12], (DEPTH, HG_WIDTH), 0.02),
        "w_out": nrm(ks[13], (DEPTH, D_MIX, D_MODEL), D_MIX ** -0.5),
        "norm2_w": 1.0 + nrm(ks[14], (DEPTH, D_MODEL), 0.02),
        "ffn_w_up": nrm(ks[15], (DEPTH, D_MODEL, 2 * D_FF), D_MODEL ** -0.5),
        "ffn_conv_w": nrm(ks[16], (DEPTH, FFN_CONV, 2 * D_FF), FFN_CONV ** -0.5),
        "ffn_conv_b": nrm(ks[17], (DEPTH, 2 * D_FF), 0.01),
        "ffn_w_down": nrm(ks[18], (DEPTH, D_FF, D_MODEL), D_FF ** -0.5),
        "final_norm_w": 1.0 + nrm(ks[19], (D_MODEL,), 0.02),
    }


def reference(x, norm1_w, w_in, lru_conv_w, lru_conv_b, lru_wa, lru_ba, lru_wx, lru_bx,
              lru_lambda, ret_norm_w, hg_lower_bounds, hg_norm_w, w_out, norm2_w,
              ffn_w_up, ffn_conv_w, ffn_conv_b, ffn_w_down, final_norm_w):
    lb_all = jnp.cumsum(jax.nn.softmax(hg_lower_bounds.astype(jnp.float32), axis=0), axis=0)
    lb_all = lb_all - lb_all[0:1]
    split_idx = list(np.cumsum(IN_SPLITS)[:-1])
    for l in range(DEPTH):
        h = rmsnorm(x, norm1_w[l])
        proj = h @ w_in[l]
        lx, lg, rq, rk, rv, rg, hq, hf, hi, hg = jnp.split(proj, split_idx, axis=-1)
        y_lru = rglru_group(lx, lg, lru_conv_w[l], lru_conv_b[l], lru_wa[l], lru_ba[l],
                            lru_wx[l], lru_bx[l], lru_lambda[l])
        y_ret = retention_group(rq, rk, rv, rg, ret_norm_w[l])
        y_hg = hgrn2_group(hq, hf, hi, hg, lb_all[l], hg_norm_w[l])
        x = x + jnp.concatenate([y_lru, y_ret, y_hg], axis=-1) @ w_out[l]
        h = rmsnorm(x, norm2_w[l])
        x = x + conv_ffn(h, ffn_w_up[l], ffn_conv_w[l], ffn_conv_b[l], ffn_w_down[l])
    return rmsnorm(x, final_norm_w)
```

```python
import functools

import jax
import jax.numpy as jnp
import numpy as np
from jax import lax
from jax.experimental import pallas as pl
from jax.experimental.pallas import tpu as pltpu

F32 = jnp.float32
BF16 = jnp.bfloat16

D_MODEL = 1024
LRU_WIDTH = 512
LRU_BLOCKS = 8
LRU_CONV = 4
LRU_C = 8.0
HEADS = 4
HEAD_DIM = 64
GROUP_WIDTH = HEADS * HEAD_DIM
ROPE_BASE = 10000.0
IN_COLS = 2 * LRU_WIDTH + 8 * GROUP_WIDTH
D_FF = 2816
FFN_CONV = 3
NORM_EPS = 1e-6

SUBLANES = 8
SEQ_TILE = 256
CHUNK = 64
LEVELS = (8, 16, 32)
FFN_COLS = 512
VMEM_LIMIT_BYTES = 56 * 1024 * 1024
NEG_BIG = -1e30


def _dot(a, b):
    return jnp.dot(a, b, preferred_element_type=F32)


def _dot_nt(a, b):
    return lax.dot_general(a, b, (((1,), (1,)), ((), ())), preferred_element_type=F32)


def _dot_tn(a, b):
    return lax.dot_general(a, b, (((0,), (0,)), ((), ())), preferred_element_type=F32)


def _rmsnorm(x, w):
    ms = jnp.mean(x * x, axis=-1, keepdims=True)
    return x * lax.rsqrt(ms + NORM_EPS) * w


def _sigmoid(x):
    return 1.0 / (1.0 + jnp.exp(-x))


def _silu(x):
    return x * _sigmoid(x)


def _gelu_tanh(x):
    return 0.5 * x * (1.0 + jnp.tanh(np.sqrt(2.0 / np.pi) * (x + 0.044715 * (x * x * x))))


def _split_dot(const_bf16, x):
    hi = x.astype(BF16)
    lo = (x - hi.astype(F32)).astype(BF16)
    return _dot(const_bf16, hi) + _dot(const_bf16, lo)


def _split_dot_rhs(x, const_bf16):
    hi = x.astype(BF16)
    lo = (x - hi.astype(F32)).astype(BF16)
    return _dot(hi, const_bf16) + _dot(lo, const_bf16)


def _tile_heads(x):
    return jnp.concatenate([x] * HEADS, axis=0)


def _constants():
    c = CHUNK
    n = np.arange(c)[:, None]
    j = np.arange(c)[None, :]
    blocks = [j <= n, j > n]
    masks = []
    hm = np.arange(GROUP_WIDTH)[None, :] % c
    for s in LEVELS:
        start = (n // (2 * s)) * (2 * s)
        anchor = start + s - 1
        second = (n - start) >= s
        blocks.append(second & (j > anchor) & (j <= n))
        blocks.append((~second) & (j > n) & (j <= anchor))
        same = (n // (2 * s)) == (hm // (2 * s))
        masks.append(same & ((n % (2 * s)) >= s) & ((hm % (2 * s)) < s))
    prefix = np.concatenate(blocks, axis=0).astype(np.float32)
    level_masks = np.stack(masks, axis=0).astype(np.float32)
    r = np.arange(GROUP_WIDTH)
    block_diag = ((r[:, None] // HEAD_DIM) == (r[None, :] // HEAD_DIM)).astype(np.float32)

    head_of_lane = r // HEAD_DIM
    log_gamma = np.log1p(-np.exp2(-5.0 - np.arange(HEADS, dtype=np.float64)))
    lg_lane = log_gamma[head_of_lane][None, :]
    pos = np.arange(c, dtype=np.float64)[:, None]
    rel = pos - hm
    intra = np.where(rel >= 0, np.exp(np.maximum(rel, 0.0) * lg_lane), 0.0)
    q_dec = np.exp((pos + 1.0) * lg_lane)
    k_dec = np.exp((c - 1.0 - pos) * lg_lane)
    chunk_dec = np.exp(c * lg_lane)
    ret_dec = np.concatenate([intra, q_dec, k_dec, np.broadcast_to(chunk_dec, (SUBLANES, GROUP_WIDTH))], axis=0)
    return dict(
        prefix=jnp.asarray(prefix, BF16),
        level_masks=jnp.asarray(level_masks, F32),
        block_diag=jnp.asarray(block_diag, F32),
        ret_dec=jnp.asarray(ret_dec, F32),
    )


def _rotary_tables(seq):
    inv = ROPE_BASE ** (-jnp.arange(0, HEAD_DIM, 2, dtype=F32) / HEAD_DIM)
    ang = jnp.arange(seq, dtype=F32)[:, None] * inv[None, :]
    cos = jnp.cos(ang)
    sin = jnp.sin(ang)
    cos_t = jnp.tile(cos, (1, 2 * HEADS))
    sin_t = jnp.tile(jnp.concatenate([-sin, sin], axis=1), (1, HEADS))
    return cos_t, sin_t


def _swap_halves(x):
    w = x.shape[-1]
    fwd = pltpu.roll(x, HEAD_DIM // 2, 1)
    bwd = pltpu.roll(x, w - HEAD_DIM // 2, 1)
    lane = lax.broadcasted_iota(jnp.int32, x.shape, 1)
    return jnp.where((lane % HEAD_DIM) < HEAD_DIM // 2, bwd, fwd)


def _head_rmsnorm(o, w, bd_bf16):
    ms = _split_dot_rhs(o * o, bd_bf16) * (1.0 / HEAD_DIM)
    return o * lax.rsqrt(ms + NORM_EPS) * w


def _mixer_kernel(layer,
                  x_ref, n1w_ref, win_ref, cw_ref, cb_ref, wa_ref, ba_ref, wx_ref, bx_ref, lam_ref,
                  rnw_ref, lb_ref, hnw_ref, wout_ref, cos_ref, sin_ref,
                  pfx_ref, lvl_ref, bd_ref, rdec_ref,
                  out_ref,
                  lxbuf, a_s, u_s, h_s, hcarry, ret_state, hg_state, o_s, b_s, kbuf, bbuf, vbuf):
    ts = SEQ_TILE
    s_idx = pl.program_id(1)

    @pl.when(s_idx == 0)
    def _():
        lxbuf[0:SUBLANES, :] = jnp.zeros((SUBLANES, LRU_WIDTH), F32)
        hcarry[...] = jnp.zeros_like(hcarry)
        ret_state[...] = jnp.zeros_like(ret_state)
        hg_state[...] = jnp.zeros_like(hg_state)
        kbuf[0:SUBLANES, :] = jnp.zeros((SUBLANES, GROUP_WIDTH), F32)
        bbuf[0:SUBLANES, :] = jnp.zeros((SUBLANES, GROUP_WIDTH), F32)
        vbuf[0:SUBLANES, :] = jnp.zeros((SUBLANES, GROUP_WIDTH), F32)

    x = x_ref[0]
    xn = _rmsnorm(x, n1w_ref[...]).astype(BF16)
    bd = bd_ref[...]
    bd_bf16 = bd.astype(BF16)

    lx = _dot(xn, win_ref[:, 0:LRU_WIDTH])
    lg = _dot(xn, win_ref[:, LRU_WIDTH:2 * LRU_WIDTH])
    lxbuf[SUBLANES:SUBLANES + ts, :] = lx
    xc = cb_ref[...] + cw_ref[3:4, :] * lx
    for k in range(LRU_CONV - 1):
        back = LRU_CONV - 1 - k
        xc = xc + cw_ref[k:k + 1, :] * lxbuf[SUBLANES - back:SUBLANES - back + ts, :]
    lxbuf[0:SUBLANES, :] = lx[ts - SUBLANES:ts, :]
    xcb = xc.astype(BF16)
    r_gate = _sigmoid(_dot(xcb, wa_ref[...]) + ba_ref[...])
    i_gate = _sigmoid(_dot(xcb, wx_ref[...]) + bx_ref[...])
    z = -lam_ref[...]
    softplus = jnp.maximum(z, 0.0) + jnp.log1p(jnp.exp(-jnp.abs(z)))
    a = jnp.exp(-LRU_C * r_gate * softplus)
    a_s[...] = a
    u_s[...] = jnp.sqrt(1.0 - a * a) * (i_gate * xc)

    row8 = lax.broadcasted_iota(jnp.int32, (SUBLANES, LRU_WIDTH), 0)

    def scan_body(g, h_prev):
        r0 = pl.multiple_of(g * SUBLANES, SUBLANES)
        av = a_s[pl.ds(r0, SUBLANES), :]
        uv = u_s[pl.ds(r0, SUBLANES), :]
        for d in (1, 2, 4):
            keep = row8 >= d
            a_sh = pltpu.roll(av, d, 0)
            u_sh = pltpu.roll(uv, d, 0)
            uv = jnp.where(keep, av * u_sh + uv, uv)
            av = jnp.where(keep, av * a_sh, av)
        h = av * h_prev + uv
        h_s[pl.ds(r0, SUBLANES), :] = h
        return h[SUBLANES - 1:SUBLANES, :]

    h_last = lax.fori_loop(0, ts // SUBLANES, scan_body, hcarry[0:1, :], unroll=4)
    hcarry[0:1, :] = h_last
    y_lru = (h_s[...] * _gelu_tanh(lg)).astype(BF16)
    acc = x + _dot(y_lru, wout_ref[0:LRU_WIDTH, :])

    pr = _dot(xn, win_ref[:, 2 * LRU_WIDTH:2 * LRU_WIDTH + 4 * GROUP_WIDTH])
    cos_t = cos_ref[...]
    sin_t = sin_ref[...]
    rq = pr[:, 0:GROUP_WIDTH]
    rk = pr[:, GROUP_WIDTH:2 * GROUP_WIDTH]
    rv = pr[:, 2 * GROUP_WIDTH:3 * GROUP_WIDTH]
    rg = pr[:, 3 * GROUP_WIDTH:4 * GROUP_WIDTH]
    q_rot = rq * cos_t + _swap_halves(rq) * sin_t
    k_rot = (rk * cos_t + _swap_halves(rk) * sin_t) * (HEAD_DIM ** -0.5)
    intra = rdec_ref[0:CHUNK, :]
    q_dec = rdec_ref[CHUNK:2 * CHUNK, :]
    k_dec = rdec_ref[2 * CHUNK:3 * CHUNK, :]
    chunk_dec = rdec_ref[3 * CHUNK:3 * CHUNK + 1, :]
    for c in range(ts // CHUNK):
        rows = slice(c * CHUNK, (c + 1) * CHUNK)
        qc = q_rot[rows]
        kc = k_rot[rows]
        vc = rv[rows]
        kt = (_tile_heads(kc) * bd).astype(BF16)
        vt = (_tile_heads(vc) * bd).astype(BF16)
        scores = _dot_nt(qc.astype(BF16), kt) * intra
        st = ret_state[...]
        o = _dot(scores.astype(BF16), vt) + _dot_nt((qc * q_dec).astype(BF16), st.astype(BF16))
        o_s[rows, :] = o
        ret_state[...] = st * chunk_dec + bd * _dot_tn(vc.astype(BF16), (kc * k_dec).astype(BF16))
    y_ret = (_head_rmsnorm(o_s[...], rnw_ref[...], bd_bf16) * _silu(rg)).astype(BF16)
    acc = acc + _dot(y_ret, wout_ref[LRU_WIDTH:LRU_WIDTH + GROUP_WIDTH, :])

    ph = _dot(xn, win_ref[:, 2 * LRU_WIDTH + 4 * GROUP_WIDTH:IN_COLS])
    hq = _silu(ph[:, 0:GROUP_WIDTH])
    hf = ph[:, GROUP_WIDTH:2 * GROUP_WIDTH]
    hv = ph[:, 2 * GROUP_WIDTH:3 * GROUP_WIDTH]
    hgate = ph[:, 3 * GROUP_WIDTH:4 * GROUP_WIDTH]
    lbp = lb_ref[...]
    depth = lbp.shape[0]
    mx = lbp[0:1, :]
    for i in range(1, depth):
        mx = jnp.maximum(mx, lbp[i:i + 1, :])
    e = [jnp.exp(lbp[i:i + 1, :] - mx) for i in range(depth)]
    den = functools.reduce(lambda p, q_: p + q_, e)
    lb = jnp.zeros((1, GROUP_WIDTH), F32)
    for i in range(1, layer + 1):
        lb = lb + e[i] / den
    log_sig = jnp.minimum(hf, 0.0) - jnp.log1p(jnp.exp(-jnp.abs(hf)))
    t_a = jnp.log(lb)
    t_b = jnp.log1p(-lb) + log_sig
    t_m = jnp.maximum(t_a, t_b)
    log_f = t_m + jnp.log(jnp.exp(t_a - t_m) + jnp.exp(t_b - t_m))
    hk = (1.0 - lb) * _sigmoid(-hf)

    pfx = pfx_ref[...]
    for c in range(ts // CHUNK):
        rows = slice(c * CHUNK, (c + 1) * CHUNK)
        ps = _split_dot(pfx, log_f[rows])
        b_in = ps[0:CHUNK]
        d_last = ps[CHUNK:2 * CHUNK]
        b_s[rows, :] = b_in
        qc = hq[rows]
        kc = hk[rows]
        vc = hv[rows]
        scores = jnp.zeros((CHUNK, HEADS * CHUNK), F32)
        for li in range(len(LEVELS)):
            dq = ps[(2 + 2 * li) * CHUNK:(3 + 2 * li) * CHUNK]
            dk = ps[(3 + 2 * li) * CHUNK:(4 + 2 * li) * CHUNK]
            ql = (qc * jnp.exp(dq)).astype(BF16)
            kl = (_tile_heads(kc * jnp.exp(dk)) * bd).astype(BF16)
            scores = scores + lvl_ref[li] * _dot_nt(ql, kl)
        vt = (_tile_heads(vc) * bd).astype(BF16)
        st = hg_state[...]
        o = _dot(scores.astype(BF16), vt) + _dot_nt((qc * jnp.exp(b_in)).astype(BF16), st.astype(BF16))
        o_s[rows, :] = o
        hg_state[...] = st * jnp.exp(b_in[CHUNK - 1:CHUNK, :]) + bd * _dot_tn(
            vc.astype(BF16), (kc * jnp.exp(d_last)).astype(BF16))

    kbuf[SUBLANES:SUBLANES + ts, :] = hk
    bbuf[SUBLANES:SUBLANES + ts, :] = b_s[...]
    vbuf[SUBLANES:SUBLANES + ts, :] = hv
    b_all = b_s[...]
    row = lax.broadcasted_iota(jnp.int32, (ts, GROUP_WIDTH), 0) % SUBLANES
    o_diag = _dot((hq * hk).astype(BF16), bd_bf16) * hv
    for j in range(1, SUBLANES):
        k_sh = kbuf[SUBLANES - j:SUBLANES - j + ts, :]
        b_sh = bbuf[SUBLANES - j:SUBLANES - j + ts, :]
        v_sh = vbuf[SUBLANES - j:SUBLANES - j + ts, :]
        valid = row >= j
        decay = jnp.exp(jnp.where(valid, b_all - b_sh, NEG_BIG))
        term = (hq * k_sh * decay).astype(BF16)
        o_diag = o_diag + _dot(term, bd_bf16) * v_sh
    o_hg = o_s[...] + o_diag
    y_hg = (_head_rmsnorm(o_hg, hnw_ref[...], bd_bf16) * _silu(hgate)).astype(BF16)
    acc = acc + _dot(y_hg, wout_ref[LRU_WIDTH + GROUP_WIDTH:D_MODEL, :])
    out_ref[0] = acc


def _const_spec(shape):
    nd = len(shape)
    return pl.BlockSpec(shape, lambda b, s, _nd=nd: (0,) * _nd)


def _mixer_call(layer, x, params, tables, consts):
    bsz, seq, d = x.shape
    ts = SEQ_TILE
    grid = (bsz, seq // ts)
    tile_spec = pl.BlockSpec((1, ts, d), lambda b, s: (b, s, 0))
    tab_spec = pl.BlockSpec((ts, GROUP_WIDTH), lambda b, s: (s, 0))
    weights = [params[k] for k in ("n1w", "win", "cw", "cb", "wa", "ba", "wx", "bx", "lam",
                                   "rnw", "lb", "hnw", "wout")]
    cvals = [consts[k] for k in ("prefix", "level_masks", "block_diag", "ret_dec")]
    in_specs = ([tile_spec] + [_const_spec(w.shape) for w in weights] + [tab_spec, tab_spec]
                + [_const_spec(c.shape) for c in cvals])
    scratch = [
        pltpu.VMEM((ts + SUBLANES, LRU_WIDTH), F32),
        pltpu.VMEM((ts, LRU_WIDTH), F32),
        pltpu.VMEM((ts, LRU_WIDTH), F32),
        pltpu.VMEM((ts, LRU_WIDTH), F32),
        pltpu.VMEM((SUBLANES, LRU_WIDTH), F32),
        pltpu.VMEM((GROUP_WIDTH, GROUP_WIDTH), F32),
        pltpu.VMEM((GROUP_WIDTH, GROUP_WIDTH), F32),
        pltpu.VMEM((ts, GROUP_WIDTH), F32),
        pltpu.VMEM((ts, GROUP_WIDTH), F32),
        pltpu.VMEM((ts + SUBLANES, GROUP_WIDTH), F32),
        pltpu.VMEM((ts + SUBLANES, GROUP_WIDTH), F32),
        pltpu.VMEM((ts + SUBLANES, GROUP_WIDTH), F32),
    ]
    return pl.pallas_call(
        functools.partial(_mixer_kernel, layer),
        grid=grid,
        in_specs=in_specs,
        out_specs=tile_spec,
        out_shape=jax.ShapeDtypeStruct(x.shape, F32),
        scratch_shapes=scratch,
        compiler_params=pltpu.CompilerParams(
            dimension_semantics=("arbitrary", "arbitrary"),
            vmem_limit_bytes=VMEM_LIMIT_BYTES),
        name=f"mixer_l{layer}",
    )(x, *weights, tables[0], tables[1], *cvals)


def _ffn_kernel(final, x_ref, n2w_ref, wup_ref, cw_ref, cb_ref, wdown_ref, fnw_ref, out_ref,
                ubuf, ucarry, act_s):
    ts = SEQ_TILE
    s_idx = pl.program_id(1)

    @pl.when(s_idx == 0)
    def _():
        ucarry[...] = jnp.zeros_like(ucarry)

    x = x_ref[0]
    h = _rmsnorm(x, n2w_ref[...]).astype(BF16)

    def conv_cols(c0, width):
        u = _dot(h, wup_ref[:, c0:c0 + width])
        ubuf[0:SUBLANES, 0:width] = ucarry[:, c0:c0 + width]
        ubuf[SUBLANES:SUBLANES + ts, 0:width] = u
        y = cb_ref[:, c0:c0 + width] + cw_ref[2:3, c0:c0 + width] * u
        y = y + cw_ref[1:2, c0:c0 + width] * ubuf[SUBLANES - 1:SUBLANES - 1 + ts, 0:width]
        y = y + cw_ref[0:1, c0:c0 + width] * ubuf[SUBLANES - 2:SUBLANES - 2 + ts, 0:width]
        ucarry[:, c0:c0 + width] = u[ts - SUBLANES:ts, :]
        return y

    c0 = 0
    while c0 < D_FF:
        width = min(FFN_COLS, D_FF - c0)
        gate = conv_cols(c0, width)
        val = conv_cols(D_FF + c0, width)
        act_s[:, c0:c0 + width] = (_silu(gate) * val).astype(BF16)
        c0 += width
    y = x + _dot(act_s[...], wdown_ref[...])
    if final:
        y = _rmsnorm(y, fnw_ref[...])
    out_ref[0] = y


def _ffn_call(final, x, n2w, wup, cw, cb, wdown, fnw):
    bsz, seq, d = x.shape
    ts = SEQ_TILE
    grid = (bsz, seq // ts)
    tile_spec = pl.BlockSpec((1, ts, d), lambda b, s: (b, s, 0))
    weights = [n2w, wup, cw, cb, wdown, fnw]
    scratch = [
        pltpu.VMEM((ts + SUBLANES, FFN_COLS), F32),
        pltpu.VMEM((SUBLANES, 2 * D_FF), F32),
        pltpu.VMEM((ts, D_FF), BF16),
    ]
    return pl.pallas_call(
        functools.partial(_ffn_kernel, final),
        grid=grid,
        in_specs=[tile_spec] + [_const_spec(w.shape) for w in weights],
        out_specs=tile_spec,
        out_shape=jax.ShapeDtypeStruct(x.shape, F32),
        scratch_shapes=scratch,
        compiler_params=pltpu.CompilerParams(
            dimension_semantics=("arbitrary", "arbitrary"),
            vmem_limit_bytes=VMEM_LIMIT_BYTES),
        name="ffn_final" if final else "ffn",
    )(x, *weights)


def _block_diag(w):
    nb, n, _ = w.shape
    eye = jnp.eye(nb, dtype=w.dtype)
    return (eye[:, None, :, None] * w[:, :, None, :]).reshape(nb * n, nb * n)


def kernel(x, norm1_w, w_in, lru_conv_w, lru_conv_b, lru_wa, lru_ba, lru_wx, lru_bx, lru_lambda,
           ret_norm_w, hg_lower_bounds, hg_norm_w, w_out, norm2_w, ffn_w_up, ffn_conv_w, ffn_conv_b,
           ffn_w_down, final_norm_w):
    bsz, seq, d = x.shape
    depth = w_in.shape[0]
    assert d == D_MODEL and seq % SEQ_TILE == 0
    consts = _constants()
    tables = _rotary_tables(seq)
    row = lambda v: v.reshape(1, -1).astype(F32)
    x = x.astype(F32)
    for l in range(depth):
        params = dict(
            n1w=row(norm1_w[l]), win=w_in[l].astype(BF16), cw=lru_conv_w[l].astype(F32),
            cb=row(lru_conv_b[l]), wa=_block_diag(lru_wa[l]).astype(BF16), ba=row(lru_ba[l]),
            wx=_block_diag(lru_wx[l]).astype(BF16), bx=row(lru_bx[l]), lam=row(lru_lambda[l]),
            rnw=row(ret_norm_w[l]), lb=hg_lower_bounds.astype(F32), hnw=row(hg_norm_w[l]),
            wout=w_out[l].astype(BF16))
        x = _mixer_call(l, x, params, tables, consts)
        x = _ffn_call(l == depth - 1, x, row(norm2_w[l]), ffn_w_up[l].astype(BF16),
                      ffn_conv_w[l].astype(F32), row(ffn_conv_b[l]), ffn_w_down[l].astype(BF16),
                      row(final_norm_w))
    return x
```

```python
import functools

import jax
import jax.numpy as jnp
import numpy as np
from jax import lax
from jax.experimental import pallas as pl
from jax.experimental.pallas import tpu as pltpu

F32 = jnp.float32
BF16 = jnp.bfloat16

D_MODEL = 1024
LRU_WIDTH = 512
LRU_BLOCKS = 8
LRU_CONV = 4
LRU_C = 8.0
HEADS = 4
HEAD_DIM = 64
GROUP_WIDTH = HEADS * HEAD_DIM
ROPE_BASE = 10000.0
IN_COLS = 2 * LRU_WIDTH + 8 * GROUP_WIDTH
D_FF = 2816
FFN_CONV = 3
NORM_EPS = 1e-6

SUBLANES = 8
SEQ_TILE = 512
CHUNK = 64
LEVELS = (8, 16, 32)
FFN_COLS = 512
VMEM_LIMIT_BYTES = 56 * 1024 * 1024
NEG_BIG = -1e30


def _dot(a, b):
    return jnp.dot(a, b, preferred_element_type=F32)


def _dot_nt(a, b):
    return lax.dot_general(a, b, (((1,), (1,)), ((), ())), preferred_element_type=F32)


def _dot_tn(a, b):
    return lax.dot_general(a, b, (((0,), (0,)), ((), ())), preferred_element_type=F32)


def _rmsnorm(x, w):
    ms = jnp.mean(x * x, axis=-1, keepdims=True)
    return x * lax.rsqrt(ms + NORM_EPS) * w


def _sigmoid(x):
    return 0.5 * jnp.tanh(0.5 * x) + 0.5


def _silu(x):
    return x * _sigmoid(x)


def _gelu_tanh(x):
    return 0.5 * x * (1.0 + jnp.tanh(np.sqrt(2.0 / np.pi) * (x + 0.044715 * (x * x * x))))


def _split_dot(const_bf16, x):
    hi = x.astype(BF16)
    lo = (x - hi.astype(F32)).astype(BF16)
    return _dot(const_bf16, hi) + _dot(const_bf16, lo)


def _split_dot_rhs(x, const_bf16):
    hi = x.astype(BF16)
    lo = (x - hi.astype(F32)).astype(BF16)
    return _dot(hi, const_bf16) + _dot(lo, const_bf16)


def _tile_heads(x, bd_bf16):
    return jnp.concatenate([x.astype(BF16)] * HEADS, axis=0) * bd_bf16


def _causal_conv(x, w_rows, b_row, head_buf, tail):
    taps = len(w_rows)
    ts, width = x.shape
    y = b_row + w_rows[taps - 1] * x
    for k in range(taps - 1):
        y = y + w_rows[k] * pltpu.roll(x, taps - 1 - k, 0)
    head_buf[0:SUBLANES, 0:width] = tail
    head_buf[SUBLANES:2 * SUBLANES, 0:width] = x[0:SUBLANES, :]
    yh = b_row + w_rows[taps - 1] * x[0:SUBLANES, :]
    for k in range(taps - 1):
        back = taps - 1 - k
        yh = yh + w_rows[k] * head_buf[SUBLANES - back:2 * SUBLANES - back, 0:width]
    return jnp.concatenate([yh, y[SUBLANES:ts, :]], axis=0)


def _constants():
    c = CHUNK
    n = np.arange(c)[:, None]
    j = np.arange(c)[None, :]
    blocks = [j <= n, j > n]
    masks = []
    hm = np.arange(GROUP_WIDTH)[None, :] % c
    for s in LEVELS:
        start = (n // (2 * s)) * (2 * s)
        anchor = start + s - 1
        second = (n - start) >= s
        blocks.append(second & (j > anchor) & (j <= n))
        blocks.append((~second) & (j > n) & (j <= anchor))
        same = (n // (2 * s)) == (hm // (2 * s))
        masks.append(same & ((n % (2 * s)) >= s) & ((hm % (2 * s)) < s))
    prefix = np.concatenate(blocks, axis=0).astype(np.float32)
    level_masks = np.stack(masks, axis=0).astype(np.float32)
    r = np.arange(GROUP_WIDTH)
    block_diag = ((r[:, None] // HEAD_DIM) == (r[None, :] // HEAD_DIM)).astype(np.float32)

    head_of_lane = r // HEAD_DIM
    log_gamma = np.log1p(-np.exp2(-5.0 - np.arange(HEADS, dtype=np.float64)))
    lg_lane = log_gamma[head_of_lane][None, :]
    pos = np.arange(c, dtype=np.float64)[:, None]
    rel = pos - hm
    intra = np.where(rel >= 0, np.exp(np.maximum(rel, 0.0) * lg_lane), 0.0)
    q_dec = np.exp((pos + 1.0) * lg_lane)
    k_dec = np.exp((c - 1.0 - pos) * lg_lane)
    chunk_dec = np.exp(c * lg_lane)
    ret_dec = np.concatenate([intra, q_dec, k_dec, np.broadcast_to(chunk_dec, (SUBLANES, GROUP_WIDTH))], axis=0)
    return dict(
        prefix=jnp.asarray(prefix, BF16),
        level_masks=jnp.asarray(level_masks, F32),
        block_diag=jnp.asarray(block_diag, F32),
        ret_dec=jnp.asarray(ret_dec, F32),
    )


def _rotary_tables(seq):
    inv = ROPE_BASE ** (-jnp.arange(0, HEAD_DIM, 2, dtype=F32) / HEAD_DIM)
    ang = jnp.arange(seq, dtype=F32)[:, None] * inv[None, :]
    cos = jnp.cos(ang)
    sin = jnp.sin(ang)
    cos_t = jnp.tile(cos, (1, 2 * HEADS))
    sin_t = jnp.tile(jnp.concatenate([-sin, sin], axis=1), (1, HEADS))
    return cos_t, sin_t


def _swap_halves(x):
    w = x.shape[-1]
    fwd = pltpu.roll(x, HEAD_DIM // 2, 1)
    bwd = pltpu.roll(x, w - HEAD_DIM // 2, 1)
    lane = lax.broadcasted_iota(jnp.int32, x.shape, 1)
    return jnp.where((lane % HEAD_DIM) < HEAD_DIM // 2, bwd, fwd)


def _head_rmsnorm(o, w, bd_bf16):
    ms = _split_dot_rhs(o * o, bd_bf16) * (1.0 / HEAD_DIM)
    return o * lax.rsqrt(ms + NORM_EPS) * w


def _mixer_kernel(layer,
                  x_ref, n1w_ref, win_ref, cw_ref, cb_ref, wa_ref, ba_ref, wx_ref, bx_ref, lam_ref,
                  rnw_ref, lb_ref, hnw_ref, wout_ref, cos_ref, sin_ref,
                  pfx_ref, lvl_ref, bd_ref, rdec_ref,
                  out_ref,
                  lxbuf, lxtail, a_s, u_s, h_s, hcarry, ret_state, hg_state, o_s, b_s, kbuf, bbuf, vbuf):
    ts = SEQ_TILE
    s_idx = pl.program_id(1)

    @pl.when(s_idx == 0)
    def _():
        lxtail[...] = jnp.zeros_like(lxtail)
        hcarry[...] = jnp.zeros_like(hcarry)
        ret_state[...] = jnp.zeros_like(ret_state)
        hg_state[...] = jnp.zeros_like(hg_state)
        kbuf[0:SUBLANES, :] = jnp.zeros((SUBLANES, GROUP_WIDTH), F32)
        bbuf[0:SUBLANES, :] = jnp.zeros((SUBLANES, GROUP_WIDTH), F32)
        vbuf[0:SUBLANES, :] = jnp.zeros((SUBLANES, GROUP_WIDTH), F32)

    x = x_ref[0]
    xn = _rmsnorm(x, n1w_ref[...]).astype(BF16)
    bd = bd_ref[...]
    bd_bf16 = bd.astype(BF16)

    lx = _dot(xn, win_ref[:, 0:LRU_WIDTH])
    lg = _dot(xn, win_ref[:, LRU_WIDTH:2 * LRU_WIDTH])
    cw = [cw_ref[k:k + 1, :] for k in range(LRU_CONV)]
    xc = _causal_conv(lx, cw, cb_ref[...], lxbuf, lxtail[...])
    lxtail[...] = lx[ts - SUBLANES:ts, :]
    xcb = xc.astype(BF16)
    r_gate = _sigmoid(_dot(xcb, wa_ref[...]) + ba_ref[...])
    i_gate = _sigmoid(_dot(xcb, wx_ref[...]) + bx_ref[...])
    z = -lam_ref[...]
    softplus = jnp.maximum(z, 0.0) + jnp.log1p(jnp.exp(-jnp.abs(z)))
    a = jnp.exp(-LRU_C * r_gate * softplus)
    a_s[...] = a
    u_s[...] = jnp.sqrt(1.0 - a * a) * (i_gate * xc)

    row8 = lax.broadcasted_iota(jnp.int32, (SUBLANES, LRU_WIDTH), 0)

    def scan_body(g, h_prev):
        r0 = pl.multiple_of(g * SUBLANES, SUBLANES)
        av = a_s[pl.ds(r0, SUBLANES), :]
        uv = u_s[pl.ds(r0, SUBLANES), :]
        for d in (1, 2, 4):
            keep = row8 >= d
            a_sh = pltpu.roll(av, d, 0)
            u_sh = pltpu.roll(uv, d, 0)
            uv = jnp.where(keep, av * u_sh + uv, uv)
            av = jnp.where(keep, av * a_sh, av)
        h = av * h_prev + uv
        h_s[pl.ds(r0, SUBLANES), :] = h
        return h[SUBLANES - 1:SUBLANES, :]

    h_last = lax.fori_loop(0, ts // SUBLANES, scan_body, hcarry[0:1, :], unroll=4)
    hcarry[0:1, :] = h_last
    y_lru = (h_s[...] * _gelu_tanh(lg)).astype(BF16)
    acc = x + _dot(y_lru, wout_ref[0:LRU_WIDTH, :])

    pr = _dot(xn, win_ref[:, 2 * LRU_WIDTH:2 * LRU_WIDTH + 4 * GROUP_WIDTH])
    cos_t = cos_ref[...]
    sin_t = sin_ref[...]
    rq = pr[:, 0:GROUP_WIDTH]
    rk = pr[:, GROUP_WIDTH:2 * GROUP_WIDTH]
    rv = pr[:, 2 * GROUP_WIDTH:3 * GROUP_WIDTH]
    rg = pr[:, 3 * GROUP_WIDTH:4 * GROUP_WIDTH]
    q_rot = rq * cos_t + _swap_halves(rq) * sin_t
    k_rot = (rk * cos_t + _swap_halves(rk) * sin_t) * (HEAD_DIM ** -0.5)
    intra = rdec_ref[0:CHUNK, :]
    q_dec = rdec_ref[CHUNK:2 * CHUNK, :]
    k_dec = rdec_ref[2 * CHUNK:3 * CHUNK, :]
    chunk_dec = rdec_ref[3 * CHUNK:3 * CHUNK + 1, :]
    for c in range(ts // CHUNK):
        rows = slice(c * CHUNK, (c + 1) * CHUNK)
        qc = q_rot[rows]
        kc = k_rot[rows]
        vc = rv[rows]
        kt = _tile_heads(kc, bd_bf16)
        vt = _tile_heads(vc, bd_bf16)
        scores = _dot_nt(qc.astype(BF16), kt) * intra
        st = ret_state[...]
        o = _dot(scores.astype(BF16), vt) + _dot_nt((qc * q_dec).astype(BF16), st.astype(BF16))
        o_s[rows, :] = o
        ret_state[...] = st * chunk_dec + bd * _dot_tn(vc.astype(BF16), (kc * k_dec).astype(BF16))
    y_ret = (_head_rmsnorm(o_s[...], rnw_ref[...], bd_bf16) * _silu(rg)).astype(BF16)
    acc = acc + _dot(y_ret, wout_ref[LRU_WIDTH:LRU_WIDTH + GROUP_WIDTH, :])

    ph = _dot(xn, win_ref[:, 2 * LRU_WIDTH + 4 * GROUP_WIDTH:IN_COLS])
    hq = _silu(ph[:, 0:GROUP_WIDTH])
    hf = ph[:, GROUP_WIDTH:2 * GROUP_WIDTH]
    hv = ph[:, 2 * GROUP_WIDTH:3 * GROUP_WIDTH]
    hgate = ph[:, 3 * GROUP_WIDTH:4 * GROUP_WIDTH]
    lbp = lb_ref[...]
    depth = lbp.shape[0]
    mx = lbp[0:1, :]
    for i in range(1, depth):
        mx = jnp.maximum(mx, lbp[i:i + 1, :])
    e = [jnp.exp(lbp[i:i + 1, :] - mx) for i in range(depth)]
    den = functools.reduce(lambda p, q_: p + q_, e)
    lb = jnp.zeros((1, GROUP_WIDTH), F32)
    for i in range(1, layer + 1):
        lb = lb + e[i] / den
    log_sig = jnp.minimum(hf, 0.0) - jnp.log1p(jnp.exp(-jnp.abs(hf)))
    t_a = jnp.log(lb)
    t_b = jnp.log1p(-lb) + log_sig
    t_m = jnp.maximum(t_a, t_b)
    log_f = t_m + jnp.log(jnp.exp(t_a - t_m) + jnp.exp(t_b - t_m))
    hk = (1.0 - lb) * _sigmoid(-hf)

    pfx = pfx_ref[...]
    for c in range(ts // CHUNK):
        rows = slice(c * CHUNK, (c + 1) * CHUNK)
        ps = _split_dot(pfx, log_f[rows])
        b_in = ps[0:CHUNK]
        d_last = ps[CHUNK:2 * CHUNK]
        b_s[rows, :] = b_in
        qc = hq[rows]
        kc = hk[rows]
        vc = hv[rows]
        scores = jnp.zeros((CHUNK, HEADS * CHUNK), F32)
        for li in range(len(LEVELS)):
            dq = ps[(2 + 2 * li) * CHUNK:(3 + 2 * li) * CHUNK]
            dk = ps[(3 + 2 * li) * CHUNK:(4 + 2 * li) * CHUNK]
            ql = (qc * jnp.exp(dq)).astype(BF16)
            kl = _tile_heads(kc * jnp.exp(dk), bd_bf16)
            scores = scores + lvl_ref[li] * _dot_nt(ql, kl)
        vt = _tile_heads(vc, bd_bf16)
        st = hg_state[...]
        o = _dot(scores.astype(BF16), vt) + _dot_nt((qc * jnp.exp(b_in)).astype(BF16), st.astype(BF16))
        o_s[rows, :] = o
        hg_state[...] = st * jnp.exp(b_in[CHUNK - 1:CHUNK, :]) + bd * _dot_tn(
            vc.astype(BF16), (kc * jnp.exp(d_last)).astype(BF16))

    kbuf[SUBLANES:SUBLANES + ts, :] = hk
    bbuf[SUBLANES:SUBLANES + ts, :] = b_s[...]
    vbuf[SUBLANES:SUBLANES + ts, :] = hv
    b_all = b_s[...]
    row = lax.broadcasted_iota(jnp.int32, (ts, GROUP_WIDTH), 0) % SUBLANES
    o_diag = _dot((hq * hk).astype(BF16), bd_bf16) * hv
    for j in range(1, SUBLANES):
        k_sh = kbuf[SUBLANES - j:SUBLANES - j + ts, :]
        b_sh = bbuf[SUBLANES - j:SUBLANES - j + ts, :]
        v_sh = vbuf[SUBLANES - j:SUBLANES - j + ts, :]
        valid = row >= j
        decay = jnp.exp(jnp.where(valid, b_all - b_sh, NEG_BIG))
        term = (hq * k_sh * decay).astype(BF16)
        o_diag = o_diag + _dot(term, bd_bf16) * v_sh
    o_hg = o_s[...] + o_diag
    y_hg = (_head_rmsnorm(o_hg, hnw_ref[...], bd_bf16) * _silu(hgate)).astype(BF16)
    acc = acc + _dot(y_hg, wout_ref[LRU_WIDTH + GROUP_WIDTH:D_MODEL, :])
    out_ref[0] = acc


def _const_spec(shape):
    nd = len(shape)
    return pl.BlockSpec(shape, lambda b, s, _nd=nd: (0,) * _nd, pipeline_mode=pl.Buffered(1))


def _mixer_call(layer, x, params, tables, consts):
    bsz, seq, d = x.shape
    ts = SEQ_TILE
    grid = (bsz, seq // ts)
    tile_spec = pl.BlockSpec((1, ts, d), lambda b, s: (b, s, 0))
    tab_spec = pl.BlockSpec((ts, GROUP_WIDTH), lambda b, s: (s, 0))
    weights = [params[k] for k in ("n1w", "win", "cw", "cb", "wa", "ba", "wx", "bx", "lam",
                                   "rnw", "lb", "hnw", "wout")]
    cvals = [consts[k] for k in ("prefix", "level_masks", "block_diag", "ret_dec")]
    in_specs = ([tile_spec] + [_const_spec(w.shape) for w in weights] + [tab_spec, tab_spec]
                + [_const_spec(c.shape) for c in cvals])
    scratch = [
        pltpu.VMEM((2 * SUBLANES, LRU_WIDTH), F32),
        pltpu.VMEM((SUBLANES, LRU_WIDTH), F32),
        pltpu.VMEM((ts, LRU_WIDTH), F32),
        pltpu.VMEM((ts, LRU_WIDTH), F32),
        pltpu.VMEM((ts, LRU_WIDTH), F32),
        pltpu.VMEM((SUBLANES, LRU_WIDTH), F32),
        pltpu.VMEM((GROUP_WIDTH, GROUP_WIDTH), F32),
        pltpu.VMEM((GROUP_WIDTH, GROUP_WIDTH), F32),
        pltpu.VMEM((ts, GROUP_WIDTH), F32),
        pltpu.VMEM((ts, GROUP_WIDTH), F32),
        pltpu.VMEM((ts + SUBLANES, GROUP_WIDTH), F32),
        pltpu.VMEM((ts + SUBLANES, GROUP_WIDTH), F32),
        pltpu.VMEM((ts + SUBLANES, GROUP_WIDTH), F32),
    ]
    return pl.pallas_call(
        functools.partial(_mixer_kernel, layer),
        grid=grid,
        in_specs=in_specs,
        out_specs=tile_spec,
        out_shape=jax.ShapeDtypeStruct(x.shape, F32),
        scratch_shapes=scratch,
        compiler_params=pltpu.CompilerParams(
            dimension_semantics=("arbitrary", "arbitrary"),
            vmem_limit_bytes=VMEM_LIMIT_BYTES),
        name=f"mixer_l{layer}",
    )(x, *weights, tables[0], tables[1], *cvals)


def _ffn_kernel(final, x_ref, n2w_ref, wup_ref, cw_ref, cb_ref, wdown_ref, fnw_ref, out_ref,
                ubuf, ucarry, act_s):
    ts = SEQ_TILE
    s_idx = pl.program_id(1)

    @pl.when(s_idx == 0)
    def _():
        ucarry[...] = jnp.zeros_like(ucarry)

    x = x_ref[0]
    h = _rmsnorm(x, n2w_ref[...]).astype(BF16)

    def conv_cols(c0, width):
        u = _dot(h, wup_ref[:, c0:c0 + width])
        cw = [cw_ref[k:k + 1, c0:c0 + width] for k in range(FFN_CONV)]
        y = _causal_conv(u, cw, cb_ref[:, c0:c0 + width], ubuf, ucarry[:, c0:c0 + width])
        ucarry[:, c0:c0 + width] = u[ts - SUBLANES:ts, :]
        return y

    c0 = 0
    while c0 < D_FF:
        width = min(FFN_COLS, D_FF - c0)
        gate = conv_cols(c0, width)
        val = conv_cols(D_FF + c0, width)
        act_s[:, c0:c0 + width] = (_silu(gate) * val).astype(BF16)
        c0 += width
    y = x + _dot(act_s[...], wdown_ref[...])
    if final:
        y = _rmsnorm(y, fnw_ref[...])
    out_ref[0] = y


def _ffn_call(final, x, n2w, wup, cw, cb, wdown, fnw):
    bsz, seq, d = x.shape
    ts = SEQ_TILE
    grid = (bsz, seq // ts)
    tile_spec = pl.BlockSpec((1, ts, d), lambda b, s: (b, s, 0))
    weights = [n2w, wup, cw, cb, wdown, fnw]
    scratch = [
        pltpu.VMEM((2 * SUBLANES, FFN_COLS), F32),
        pltpu.VMEM((SUBLANES, 2 * D_FF), F32),
        pltpu.VMEM((ts, D_FF), BF16),
    ]
    return pl.pallas_call(
        functools.partial(_ffn_kernel, final),
        grid=grid,
        in_specs=[tile_spec] + [_const_spec(w.shape) for w in weights],
        out_specs=tile_spec,
        out_shape=jax.ShapeDtypeStruct(x.shape, F32),
        scratch_shapes=scratch,
        compiler_params=pltpu.CompilerParams(
            dimension_semantics=("arbitrary", "arbitrary"),
            vmem_limit_bytes=VMEM_LIMIT_BYTES),
        name="ffn_final" if final else "ffn",
    )(x, *weights)


def _block_diag(w):
    nb, n, _ = w.shape
    eye = jnp.eye(nb, dtype=w.dtype)
    return (eye[:, None, :, None] * w[:, :, None, :]).reshape(nb * n, nb * n)


def kernel(x, norm1_w, w_in, lru_conv_w, lru_conv_b, lru_wa, lru_ba, lru_wx, lru_bx, lru_lambda,
           ret_norm_w, hg_lower_bounds, hg_norm_w, w_out, norm2_w, ffn_w_up, ffn_conv_w, ffn_conv_b,
           ffn_w_down, final_norm_w):
    bsz, seq, d = x.shape
    depth = w_in.shape[0]
    assert d == D_MODEL and seq % SEQ_TILE == 0
    consts = _constants()
    tables = _rotary_tables(seq)
    row = lambda v: v.reshape(1, -1).astype(F32)
    x = x.astype(F32)
    for l in range(depth):
        params = dict(
            n1w=row(norm1_w[l]), win=w_in[l].astype(BF16), cw=lru_conv_w[l].astype(F32),
            cb=row(lru_conv_b[l]), wa=_block_diag(lru_wa[l]).astype(BF16), ba=row(lru_ba[l]),
            wx=_block_diag(lru_wx[l]).astype(BF16), bx=row(lru_bx[l]), lam=row(lru_lambda[l]),
            rnw=row(ret_norm_w[l]), lb=hg_lower_bounds.astype(F32), hnw=row(hg_norm_w[l]),
            wout=w_out[l].astype(BF16))
        x = _mixer_call(l, x, params, tables, consts)
        x = _ffn_call(l == depth - 1, x, row(norm2_w[l]), ffn_w_up[l].astype(BF16),
                      ffn_conv_w[l].astype(F32), row(ffn_conv_b[l]), ffn_w_down[l].astype(BF16),
                      row(final_norm_w))
    return x
```

```python
import functools

import jax
import jax.numpy as jnp
import numpy as np
from jax import lax
from jax.experimental import pallas as pl
from jax.experimental.pallas import tpu as pltpu

F32 = jnp.float32
BF16 = jnp.bfloat16

D_MODEL = 1024
LRU_WIDTH = 512
LRU_BLOCKS = 8
LRU_CONV = 4
LRU_C = 8.0
HEADS = 4
HEAD_DIM = 64
GROUP_WIDTH = HEADS * HEAD_DIM
ROPE_BASE = 10000.0
IN_COLS = 2 * LRU_WIDTH + 8 * GROUP_WIDTH
D_FF = 2816
FFN_CONV = 3
NORM_EPS = 1e-6

SUBLANES = 8
SEQ_TILE = 512
CHUNK = 64
LEVELS = (1, 2, 4, 8, 16, 32)
PAIR = 2
PAIR_WIDTH = PAIR * HEAD_DIM
FFN_COLS = 512
VMEM_LIMIT_BYTES = 56 * 1024 * 1024


def _dot(a, b):
    return jnp.dot(a, b, preferred_element_type=F32)


def _dot_nt(a, b):
    return lax.dot_general(a, b, (((1,), (1,)), ((), ())), preferred_element_type=F32)


def _dot_tn(a, b):
    return lax.dot_general(a, b, (((0,), (0,)), ((), ())), preferred_element_type=F32)


def _rmsnorm(x, w):
    ms = jnp.mean(x * x, axis=-1, keepdims=True)
    return x * lax.rsqrt(ms + NORM_EPS) * w


def _sigmoid(x):
    return 0.5 * jnp.tanh(0.5 * x) + 0.5


def _silu(x):
    return x * _sigmoid(x)


def _gelu_tanh(x):
    return 0.5 * x * (1.0 + jnp.tanh(np.sqrt(2.0 / np.pi) * (x + 0.044715 * (x * x * x))))


def _split_dot_rhs(x, const_bf16):
    hi = x.astype(BF16)
    lo = (x - hi.astype(F32)).astype(BF16)
    return _dot(hi, const_bf16) + _dot(lo, const_bf16)


def _attention_tile(chunks, state_ref, bd_pair, out_ref):
    c = CHUNK
    pairs = range(HEADS // PAIR)
    lanes = [slice(p * PAIR_WIDTH, (p + 1) * PAIR_WIDTH) for p in pairs]
    low = lax.broadcasted_iota(jnp.int32, (c, PAIR_WIDTH), 1) < HEAD_DIM

    scores = []
    for levels, _, _, _, _ in chunks:
        per_pair = []
        for p in pairs:
            sc = None
            for ql, kl, mask in levels:
                qp = ql[:, lanes[p]]
                zero = jnp.zeros_like(qp)
                lhs = jnp.concatenate([jnp.where(low, qp, zero), jnp.where(low, zero, qp)], axis=0)
                t = (mask if mask.ndim == 2 else mask[p]) * _dot_nt(lhs, kl[:, lanes[p]])
                sc = t if sc is None else sc + t
            per_pair.append(sc.astype(BF16))
        scores.append(per_pair)

    pv, upd = [], []
    for (_, vc, _, k_out, _), per_pair in zip(chunks, scores):
        vps = [vc[:, lanes[p]].astype(BF16) for p in pairs]
        pv.append([_dot(per_pair[p], vps[p]) for p in pairs])
        upd.append([bd_pair * _dot_tn(vps[p], k_out[:, lanes[p]].astype(BF16)) for p in pairs])

    outs = [[None] * len(lanes) for _ in chunks]
    for p in pairs:
        st = state_ref[p]
        for ci, (_, _, q_in, _, dec) in enumerate(chunks):
            o2 = pv[ci][p]
            outs[ci][p] = (jnp.where(low, o2[0:c], o2[c:2 * c])
                           + _dot_nt(q_in[:, lanes[p]].astype(BF16), st.astype(BF16)))
            st = st * dec[:, lanes[p]] + upd[ci][p]
        state_ref[p] = st
    for ci in range(len(chunks)):
        out_ref[ci * c:(ci + 1) * c, :] = jnp.concatenate(outs[ci], axis=1)


def _causal_conv(x, w_rows, b_row, head_buf, tail):
    taps = len(w_rows)
    ts, width = x.shape
    y = b_row + w_rows[taps - 1] * x
    for k in range(taps - 1):
        y = y + w_rows[k] * pltpu.roll(x, taps - 1 - k, 0)
    head_buf[0:SUBLANES, 0:width] = tail
    head_buf[SUBLANES:2 * SUBLANES, 0:width] = x[0:SUBLANES, :]
    yh = b_row + w_rows[taps - 1] * x[0:SUBLANES, :]
    for k in range(taps - 1):
        back = taps - 1 - k
        yh = yh + w_rows[k] * head_buf[SUBLANES - back:2 * SUBLANES - back, 0:width]
    return jnp.concatenate([yh, y[SUBLANES:ts, :]], axis=0)


def _constants():
    c = CHUNK
    n = np.arange(c)[:, None]
    j = np.arange(c)[None, :]
    blocks = [j <= n, j > n]
    masks = []
    n2 = np.arange(PAIR * c)[:, None] % c
    for s in LEVELS:
        start = (n // (2 * s)) * (2 * s)
        anchor = start + s - 1
        second = (n - start) >= s
        blocks.append(second & (j > anchor) & (j <= n))
        blocks.append((~second) & (j > n) & (j <= anchor))
        same = (n2 // (2 * s)) == (j // (2 * s))
        masks.append(same & ((n2 % (2 * s)) >= s) & ((j % (2 * s)) < s))
    masks.append(n2 == j)
    prefix = np.concatenate(blocks, axis=0).astype(np.float32)
    prefix2 = np.concatenate([prefix, prefix], axis=1)
    level_masks = np.stack(masks, axis=0).astype(np.float32)
    r = np.arange(GROUP_WIDTH)
    block_diag = ((r[:, None] // HEAD_DIM) == (r[None, :] // HEAD_DIM)).astype(np.float32)

    log_gamma = np.log1p(-np.exp2(-5.0 - np.arange(HEADS, dtype=np.float64)))
    lg_lane = log_gamma[r // HEAD_DIM][None, :]
    pos = np.arange(c, dtype=np.float64)[:, None]
    rel = (n2 - j).astype(np.float64)
    intra = []
    for p in range(HEADS // PAIR):
        lg_row = log_gamma[PAIR * p + np.arange(PAIR * c) // c][:, None]
        intra.append(np.where(rel >= 0, np.exp(np.maximum(rel, 0.0) * lg_row), 0.0))
    q_dec = np.exp((pos + 1.0) * lg_lane)
    k_dec = np.exp((c - 1.0 - pos) * lg_lane)
    chunk_dec = np.exp(c * lg_lane)
    ret_dec = np.concatenate([q_dec, k_dec, np.broadcast_to(chunk_dec, (SUBLANES, GROUP_WIDTH))], axis=0)
    return dict(
        prefix=jnp.asarray(prefix2, BF16),
        level_masks=jnp.asarray(level_masks, F32),
        block_diag=jnp.asarray(block_diag, F32),
        ret_intra=jnp.asarray(np.stack(intra, axis=0), F32),
        ret_dec=jnp.asarray(ret_dec, F32),
    )


def _rotary_tables(seq):
    inv = ROPE_BASE ** (-jnp.arange(0, HEAD_DIM, 2, dtype=F32) / HEAD_DIM)
    ang = jnp.arange(seq, dtype=F32)[:, None] * inv[None, :]
    cos = jnp.cos(ang)
    sin = jnp.sin(ang)
    cos_t = jnp.tile(cos, (1, 2 * HEADS))
    sin_t = jnp.tile(jnp.concatenate([-sin, sin], axis=1), (1, HEADS))
    return cos_t, sin_t


def _swap_halves(x):
    w = x.shape[-1]
    fwd = pltpu.roll(x, HEAD_DIM // 2, 1)
    bwd = pltpu.roll(x, w - HEAD_DIM // 2, 1)
    lane = lax.broadcasted_iota(jnp.int32, x.shape, 1)
    return jnp.where((lane % HEAD_DIM) < HEAD_DIM // 2, bwd, fwd)


def _head_rmsnorm(o, w, bd_bf16):
    ms = _split_dot_rhs(o * o, bd_bf16) * (1.0 / HEAD_DIM)
    return o * lax.rsqrt(ms + NORM_EPS) * w


def _mixer_kernel(layer,
                  x_ref, n1w_ref, win_ref, cw_ref, cb_ref, wa_ref, ba_ref, wx_ref, bx_ref, lam_ref,
                  rnw_ref, lb_ref, hnw_ref, wout_ref, cos_ref, sin_ref,
                  pfx_ref, lvl_ref, bd_ref, rint_ref, rdec_ref,
                  out_ref,
                  lxbuf, lxtail, a_s, u_s, h_s, hcarry, ret_state, hg_state, o_s):
    ts = SEQ_TILE
    s_idx = pl.program_id(1)

    @pl.when(s_idx == 0)
    def _():
        lxtail[...] = jnp.zeros_like(lxtail)
        hcarry[...] = jnp.zeros_like(hcarry)
        ret_state[...] = jnp.zeros_like(ret_state)
        hg_state[...] = jnp.zeros_like(hg_state)

    x = x_ref[0]
    xn = _rmsnorm(x, n1w_ref[...]).astype(BF16)
    bd = bd_ref[...]
    bd_bf16 = bd.astype(BF16)

    lx = _dot(xn, win_ref[:, 0:LRU_WIDTH])
    lg = _dot(xn, win_ref[:, LRU_WIDTH:2 * LRU_WIDTH])
    cw = [cw_ref[k:k + 1, :] for k in range(LRU_CONV)]
    xc = _causal_conv(lx, cw, cb_ref[...], lxbuf, lxtail[...])
    lxtail[...] = lx[ts - SUBLANES:ts, :]
    xcb = xc.astype(BF16)
    r_gate = _sigmoid(_dot(xcb, wa_ref[...]) + ba_ref[...])
    i_gate = _sigmoid(_dot(xcb, wx_ref[...]) + bx_ref[...])
    z = -lam_ref[...]
    softplus = jnp.maximum(z, 0.0) + jnp.log1p(jnp.exp(-jnp.abs(z)))
    a = jnp.exp(-LRU_C * r_gate * softplus)
    a_s[...] = a
    u_s[...] = jnp.sqrt(1.0 - a * a) * (i_gate * xc)

    row8 = lax.broadcasted_iota(jnp.int32, (SUBLANES, LRU_WIDTH), 0)

    def scan_body(g, h_prev):
        r0 = pl.multiple_of(g * SUBLANES, SUBLANES)
        av = a_s[pl.ds(r0, SUBLANES), :]
        uv = u_s[pl.ds(r0, SUBLANES), :]
        for d in (1, 2, 4):
            keep = row8 >= d
            a_sh = pltpu.roll(av, d, 0)
            u_sh = pltpu.roll(uv, d, 0)
            uv = jnp.where(keep, av * u_sh + uv, uv)
            av = jnp.where(keep, av * a_sh, av)
        h = av * h_prev + uv
        h_s[pl.ds(r0, SUBLANES), :] = h
        return h[SUBLANES - 1:SUBLANES, :]

    h_last = lax.fori_loop(0, ts // SUBLANES, scan_body, hcarry[0:1, :], unroll=4)
    hcarry[0:1, :] = h_last
    y_lru = (h_s[...] * _gelu_tanh(lg)).astype(BF16)
    acc = x + _dot(y_lru, wout_ref[0:LRU_WIDTH, :])

    pr = _dot(xn, win_ref[:, 2 * LRU_WIDTH:2 * LRU_WIDTH + 4 * GROUP_WIDTH])
    cos_t = cos_ref[...]
    sin_t = sin_ref[...]
    rq = pr[:, 0:GROUP_WIDTH]
    rk = pr[:, GROUP_WIDTH:2 * GROUP_WIDTH]
    rv = pr[:, 2 * GROUP_WIDTH:3 * GROUP_WIDTH]
    rg = pr[:, 3 * GROUP_WIDTH:4 * GROUP_WIDTH]
    q_rot = rq * cos_t + _swap_halves(rq) * sin_t
    k_rot = (rk * cos_t + _swap_halves(rk) * sin_t) * (HEAD_DIM ** -0.5)
    q_dec = rdec_ref[0:CHUNK, :]
    k_dec = rdec_ref[CHUNK:2 * CHUNK, :]
    chunk_dec = rdec_ref[2 * CHUNK:2 * CHUNK + 1, :]
    bd_pair = bd[0:PAIR_WIDTH, 0:PAIR_WIDTH]
    intra = rint_ref[...]
    chunks = []
    for c in range(ts // CHUNK):
        rows = slice(c * CHUNK, (c + 1) * CHUNK)
        qc = q_rot[rows]
        kc = k_rot[rows]
        chunks.append(([(qc.astype(BF16), kc.astype(BF16), intra)], rv[rows], qc * q_dec, kc * k_dec, chunk_dec))
    _attention_tile(chunks, ret_state, bd_pair, o_s)
    y_ret = (_head_rmsnorm(o_s[...], rnw_ref[...], bd_bf16) * _silu(rg)).astype(BF16)
    acc = acc + _dot(y_ret, wout_ref[LRU_WIDTH:LRU_WIDTH + GROUP_WIDTH, :])

    ph = _dot(xn, win_ref[:, 2 * LRU_WIDTH + 4 * GROUP_WIDTH:IN_COLS])
    hq = _silu(ph[:, 0:GROUP_WIDTH])
    hf = ph[:, GROUP_WIDTH:2 * GROUP_WIDTH]
    hv = ph[:, 2 * GROUP_WIDTH:3 * GROUP_WIDTH]
    hgate = ph[:, 3 * GROUP_WIDTH:4 * GROUP_WIDTH]
    lbp = lb_ref[...]
    depth = lbp.shape[0]
    mx = lbp[0:1, :]
    for i in range(1, depth):
        mx = jnp.maximum(mx, lbp[i:i + 1, :])
    e = [jnp.exp(lbp[i:i + 1, :] - mx) for i in range(depth)]
    den = functools.reduce(lambda p, q_: p + q_, e)
    lb = jnp.zeros((1, GROUP_WIDTH), F32)
    for i in range(1, layer + 1):
        lb = lb + e[i] / den
    log_sig = jnp.minimum(hf, 0.0) - jnp.log1p(jnp.exp(-jnp.abs(hf)))
    t_a = jnp.log(lb)
    t_b = jnp.log1p(-lb) + log_sig
    t_m = jnp.maximum(t_a, t_b)
    log_f = t_m + jnp.log(jnp.exp(t_a - t_m) + jnp.exp(t_b - t_m))
    hk = (1.0 - lb) * _sigmoid(-hf)

    pfx = pfx_ref[...]
    n_lvl = len(LEVELS)
    sums = []
    for c in range(ts // CHUNK):
        gc = log_f[c * CHUNK:(c + 1) * CHUNK]
        g_hi = gc.astype(BF16)
        g_lo = (gc - g_hi.astype(F32)).astype(BF16)
        sums.append(_dot(pfx, jnp.concatenate([g_hi, g_lo], axis=0)))
    chunks = []
    for c, ps in enumerate(sums):
        rows = slice(c * CHUNK, (c + 1) * CHUNK)
        b_in = ps[0:CHUNK]
        d_last = ps[CHUNK:2 * CHUNK]
        qc = hq[rows]
        kc = hk[rows]
        levels = []
        for li in range(n_lvl):
            dq = ps[(2 + 2 * li) * CHUNK:(3 + 2 * li) * CHUNK]
            dk = ps[(3 + 2 * li) * CHUNK:(4 + 2 * li) * CHUNK]
            levels.append(((qc * jnp.exp(dq)).astype(BF16), (kc * jnp.exp(dk)).astype(BF16), lvl_ref[li]))
        levels.append((qc.astype(BF16), kc.astype(BF16), lvl_ref[n_lvl]))
        chunks.append((levels, hv[rows], qc * jnp.exp(b_in), kc * jnp.exp(d_last),
                       jnp.exp(b_in[CHUNK - 1:CHUNK, :])))
    _attention_tile(chunks, hg_state, bd_pair, o_s)
    y_hg = (_head_rmsnorm(o_s[...], hnw_ref[...], bd_bf16) * _silu(hgate)).astype(BF16)
    acc = acc + _dot(y_hg, wout_ref[LRU_WIDTH + GROUP_WIDTH:D_MODEL, :])
    out_ref[0] = acc


def _const_spec(shape):
    nd = len(shape)
    return pl.BlockSpec(shape, lambda b, s, _nd=nd: (0,) * _nd, pipeline_mode=pl.Buffered(1))


def _mixer_call(layer, x, params, tables, consts):
    bsz, seq, d = x.shape
    ts = SEQ_TILE
    grid = (bsz, seq // ts)
    tile_spec = pl.BlockSpec((1, ts, d), lambda b, s: (b, s, 0))
    tab_spec = pl.BlockSpec((ts, GROUP_WIDTH), lambda b, s: (s, 0))
    weights = [params[k] for k in ("n1w", "win", "cw", "cb", "wa", "ba", "wx", "bx", "lam",
                                   "rnw", "lb", "hnw", "wout")]
    cvals = [consts[k] for k in ("prefix", "level_masks", "block_diag", "ret_intra", "ret_dec")]
    in_specs = ([tile_spec] + [_const_spec(w.shape) for w in weights] + [tab_spec, tab_spec]
                + [_const_spec(c.shape) for c in cvals])
    scratch = [
        pltpu.VMEM((2 * SUBLANES, LRU_WIDTH), F32),
        pltpu.VMEM((SUBLANES, LRU_WIDTH), F32),
        pltpu.VMEM((ts, LRU_WIDTH), F32),
        pltpu.VMEM((ts, LRU_WIDTH), F32),
        pltpu.VMEM((ts, LRU_WIDTH), F32),
        pltpu.VMEM((SUBLANES, LRU_WIDTH), F32),
        pltpu.VMEM((HEADS // PAIR, PAIR_WIDTH, PAIR_WIDTH), F32),
        pltpu.VMEM((HEADS // PAIR, PAIR_WIDTH, PAIR_WIDTH), F32),
        pltpu.VMEM((ts, GROUP_WIDTH), F32),
    ]
    return pl.pallas_call(
        functools.partial(_mixer_kernel, layer),
        grid=grid,
        in_specs=in_specs,
        out_specs=tile_spec,
        out_shape=jax.ShapeDtypeStruct(x.shape, F32),
        scratch_shapes=scratch,
        compiler_params=pltpu.CompilerParams(
            dimension_semantics=("arbitrary", "arbitrary"),
            vmem_limit_bytes=VMEM_LIMIT_BYTES),
        name=f"mixer_l{layer}",
    )(x, *weights, tables[0], tables[1], *cvals)


def _ffn_kernel(final, x_ref, n2w_ref, wup_ref, cw_ref, cb_ref, wdown_ref, fnw_ref, out_ref,
                ubuf, ucarry, act_s):
    ts = SEQ_TILE
    s_idx = pl.program_id(1)

    @pl.when(s_idx == 0)
    def _():
        ucarry[...] = jnp.zeros_like(ucarry)

    x = x_ref[0]
    h = _rmsnorm(x, n2w_ref[...]).astype(BF16)

    def conv_cols(c0, width):
        u = _dot(h, wup_ref[:, c0:c0 + width])
        cw = [cw_ref[k:k + 1, c0:c0 + width] for k in range(FFN_CONV)]
        y = _causal_conv(u, cw, cb_ref[:, c0:c0 + width], ubuf, ucarry[:, c0:c0 + width])
        ucarry[:, c0:c0 + width] = u[ts - SUBLANES:ts, :]
        return y

    c0 = 0
    while c0 < D_FF:
        width = min(FFN_COLS, D_FF - c0)
        gate = conv_cols(c0, width)
        val = conv_cols(D_FF + c0, width)
        act_s[:, c0:c0 + width] = (_silu(gate) * val).astype(BF16)
        c0 += width
    y = x + _dot(act_s[...], wdown_ref[...])
    if final:
        y = _rmsnorm(y, fnw_ref[...])
    out_ref[0] = y


def _ffn_call(final, x, n2w, wup, cw, cb, wdown, fnw):
    bsz, seq, d = x.shape
    ts = SEQ_TILE
    grid = (bsz, seq // ts)
    tile_spec = pl.BlockSpec((1, ts, d), lambda b, s: (b, s, 0))
    weights = [n2w, wup, cw, cb, wdown, fnw]
    scratch = [
        pltpu.VMEM((2 * SUBLANES, FFN_COLS), F32),
        pltpu.VMEM((SUBLANES, 2 * D_FF), F32),
        pltpu.VMEM((ts, D_FF), BF16),
    ]
    return pl.pallas_call(
        functools.partial(_ffn_kernel, final),
        grid=grid,
        in_specs=[tile_spec] + [_const_spec(w.shape) for w in weights],
        out_specs=tile_spec,
        out_shape=jax.ShapeDtypeStruct(x.shape, F32),
        scratch_shapes=scratch,
        compiler_params=pltpu.CompilerParams(
            dimension_semantics=("arbitrary", "arbitrary"),
            vmem_limit_bytes=VMEM_LIMIT_BYTES),
        name="ffn_final" if final else "ffn",
    )(x, *weights)


def _block_diag(w):
    nb, n, _ = w.shape
    eye = jnp.eye(nb, dtype=w.dtype)
    return (eye[:, None, :, None] * w[:, :, None, :]).reshape(nb * n, nb * n)


def kernel(x, norm1_w, w_in, lru_conv_w, lru_conv_b, lru_wa, lru_ba, lru_wx, lru_bx, lru_lambda,
           ret_norm_w, hg_lower_bounds, hg_norm_w, w_out, norm2_w, ffn_w_up, ffn_conv_w, ffn_conv_b,
           ffn_w_down, final_norm_w):
    bsz, seq, d = x.shape
    depth = w_in.shape[0]
    assert d == D_MODEL and seq % SEQ_TILE == 0
    consts = _constants()
    tables = _rotary_tables(seq)
    row = lambda v: v.reshape(1, -1).astype(F32)
    x = x.astype(F32)
    for l in range(depth):
        params = dict(
            n1w=row(norm1_w[l]), win=w_in[l].astype(BF16), cw=lru_conv_w[l].astype(F32),
            cb=row(lru_conv_b[l]), wa=_block_diag(lru_wa[l]).astype(BF16), ba=row(lru_ba[l]),
            wx=_block_diag(lru_wx[l]).astype(BF16), bx=row(lru_bx[l]), lam=row(lru_lambda[l]),
            rnw=row(ret_norm_w[l]), lb=hg_lower_bounds.astype(F32), hnw=row(hg_norm_w[l]),
            wout=w_out[l].astype(BF16))
        x = _mixer_call(l, x, params, tables, consts)
        x = _ffn_call(l == depth - 1, x, row(norm2_w[l]), ffn_w_up[l].astype(BF16),
                      ffn_conv_w[l].astype(F32), row(ffn_conv_b[l]), ffn_w_down[l].astype(BF16),
                      row(final_norm_w))
    return x
```

```python
import functools

import jax
import jax.numpy as jnp
import numpy as np
from jax import lax
from jax.experimental import pallas as pl
from jax.experimental.pallas import tpu as pltpu

F32 = jnp.float32
BF16 = jnp.bfloat16

D_MODEL = 1024
LRU_WIDTH = 512
LRU_BLOCKS = 8
LRU_CONV = 4
LRU_C = 8.0
HEADS = 4
HEAD_DIM = 64
GROUP_WIDTH = HEADS * HEAD_DIM
ROPE_BASE = 10000.0
IN_COLS = 2 * LRU_WIDTH + 8 * GROUP_WIDTH
D_FF = 2816
FFN_CONV = 3
NORM_EPS = 1e-6

SUBLANES = 8
SEQ_TILE = 512
FFN_SEQ_TILE = 1024
ROW_BLOCK = 256
CHUNK = 64
LEVELS = (1, 2, 4, 8, 16, 32)
PAIR = 2
PAIR_WIDTH = PAIR * HEAD_DIM
FFN_COLS = 512
VMEM_LIMIT_BYTES = 56 * 1024 * 1024


def _dot(a, b):
    return jnp.dot(a, b, preferred_element_type=F32)


def _dot_nt(a, b):
    return lax.dot_general(a, b, (((1,), (1,)), ((), ())), preferred_element_type=F32)


def _dot_tn(a, b):
    return lax.dot_general(a, b, (((0,), (0,)), ((), ())), preferred_element_type=F32)


def _rmsnorm(x, w):
    ms = jnp.mean(x * x, axis=-1, keepdims=True)
    return x * lax.rsqrt(ms + NORM_EPS) * w


def _sigmoid(x):
    return 0.5 * jnp.tanh(0.5 * x) + 0.5


def _silu(x):
    return x * _sigmoid(x)


def _gelu_tanh(x):
    return 0.5 * x * (1.0 + jnp.tanh(np.sqrt(2.0 / np.pi) * (x + 0.044715 * (x * x * x))))


def _split_dot_rhs(x, const_bf16):
    hi = x.astype(BF16)
    lo = (x - hi.astype(F32)).astype(BF16)
    return _dot(hi, const_bf16) + _dot(lo, const_bf16)


_PAIR_LANES = [slice(p * PAIR_WIDTH, (p + 1) * PAIR_WIDTH) for p in range(HEADS // PAIR)]


def _low_head_lanes():
    return lax.broadcasted_iota(jnp.int32, (CHUNK, PAIR_WIDTH), 1) < HEAD_DIM


def _attn_scores(chunks):
    low = _low_head_lanes()
    scores = []
    for levels, _, _, _, _ in chunks:
        per_pair = []
        for p, ln in enumerate(_PAIR_LANES):
            sc = None
            for ql, kl, mask in levels:
                qp = ql[:, ln]
                zero = jnp.zeros_like(qp)
                lhs = jnp.concatenate([jnp.where(low, qp, zero), jnp.where(low, zero, qp)], axis=0)
                t = (mask if mask.ndim == 2 else mask[p]) * _dot_nt(lhs, kl[:, ln])
                sc = t if sc is None else sc + t
            per_pair.append(sc.astype(BF16))
        scores.append(per_pair)
    return scores


def _attn_values(chunks, scores, bd_pair):
    pv, upd = [], []
    for (_, vc, _, k_out, _), per_pair in zip(chunks, scores):
        vps = [vc[:, ln].astype(BF16) for ln in _PAIR_LANES]
        pv.append([_dot(sc, vp) for sc, vp in zip(per_pair, vps)])
        upd.append([bd_pair * _dot_tn(vp, k_out[:, ln].astype(BF16)) for vp, ln in zip(vps, _PAIR_LANES)])
    return pv, upd


def _attn_finish(chunks, pv, upd, state_ref, out_ref):
    c = CHUNK
    low = _low_head_lanes()
    outs = [[None] * len(_PAIR_LANES) for _ in chunks]
    for p, ln in enumerate(_PAIR_LANES):
        st = state_ref[p]
        for ci, (_, _, q_in, _, dec) in enumerate(chunks):
            o2 = pv[ci][p]
            outs[ci][p] = (jnp.where(low, o2[0:c], o2[c:2 * c])
                           + _dot_nt(q_in[:, ln].astype(BF16), st.astype(BF16)))
            st = st * dec[:, ln] + upd[ci][p]
        state_ref[p] = st
    for ci in range(len(chunks)):
        out_ref[ci * c:(ci + 1) * c, :] = jnp.concatenate(outs[ci], axis=1)


def _causal_conv(x, w_rows, b_row, head_buf, tail):
    taps = len(w_rows)
    ts, width = x.shape
    y = b_row + w_rows[taps - 1] * x
    for k in range(taps - 1):
        y = y + w_rows[k] * pltpu.roll(x, taps - 1 - k, 0)
    head_buf[0:SUBLANES, 0:width] = tail
    head_buf[SUBLANES:2 * SUBLANES, 0:width] = x[0:SUBLANES, :]
    yh = b_row + w_rows[taps - 1] * x[0:SUBLANES, :]
    for k in range(taps - 1):
        back = taps - 1 - k
        yh = yh + w_rows[k] * head_buf[SUBLANES - back:2 * SUBLANES - back, 0:width]
    return jnp.concatenate([yh, y[SUBLANES:ts, :]], axis=0)


def _constants():
    c = CHUNK
    n = np.arange(c)[:, None]
    j = np.arange(c)[None, :]
    blocks = [j <= n, j > n]
    masks = []
    n2 = np.arange(PAIR * c)[:, None] % c
    for s in LEVELS:
        start = (n // (2 * s)) * (2 * s)
        anchor = start + s - 1
        second = (n - start) >= s
        blocks.append((second & (j > anchor) & (j <= n)) | ((~second) & (j > n) & (j <= anchor)))
        same = (n2 // (2 * s)) == (j // (2 * s))
        masks.append(same & ((n2 % (2 * s)) >= s) & ((j % (2 * s)) < s))
    masks.append(n2 == j)
    prefix = np.concatenate(blocks, axis=0).astype(np.float32)
    prefix2 = np.concatenate([prefix, prefix], axis=1)
    level_masks = np.stack(masks, axis=0).astype(np.float32)
    r = np.arange(GROUP_WIDTH)
    block_diag = ((r[:, None] // HEAD_DIM) == (r[None, :] // HEAD_DIM)).astype(np.float32)

    log_gamma = np.log1p(-np.exp2(-5.0 - np.arange(HEADS, dtype=np.float64)))
    lg_lane = log_gamma[r // HEAD_DIM][None, :]
    pos = np.arange(c, dtype=np.float64)[:, None]
    rel = (n2 - j).astype(np.float64)
    intra = []
    for p in range(HEADS // PAIR):
        lg_row = log_gamma[PAIR * p + np.arange(PAIR * c) // c][:, None]
        intra.append(np.where(rel >= 0, np.exp(np.maximum(rel, 0.0) * lg_row), 0.0))
    q_dec = np.exp((pos + 1.0) * lg_lane)
    k_dec = np.exp((c - 1.0 - pos) * lg_lane)
    chunk_dec = np.exp(c * lg_lane)
    ret_dec = np.concatenate([q_dec, k_dec, np.broadcast_to(chunk_dec, (SUBLANES, GROUP_WIDTH))], axis=0)
    return dict(
        prefix=jnp.asarray(prefix2, BF16),
        level_masks=jnp.asarray(level_masks, F32),
        block_diag=jnp.asarray(block_diag, F32),
        ret_intra=jnp.asarray(np.stack(intra, axis=0), F32),
        ret_dec=jnp.asarray(ret_dec, F32),
    )


def _rotary_tables(seq):
    inv = ROPE_BASE ** (-jnp.arange(0, HEAD_DIM, 2, dtype=F32) / HEAD_DIM)
    ang = jnp.arange(seq, dtype=F32)[:, None] * inv[None, :]
    cos = jnp.cos(ang)
    sin = jnp.sin(ang)
    cos_t = jnp.tile(cos, (1, 2 * HEADS))
    sin_t = jnp.tile(jnp.concatenate([-sin, sin], axis=1), (1, HEADS))
    return cos_t, sin_t


def _swap_halves(x):
    w = x.shape[-1]
    fwd = pltpu.roll(x, HEAD_DIM // 2, 1)
    bwd = pltpu.roll(x, w - HEAD_DIM // 2, 1)
    lane = lax.broadcasted_iota(jnp.int32, x.shape, 1)
    return jnp.where((lane % HEAD_DIM) < HEAD_DIM // 2, bwd, fwd)


def _head_rmsnorm(o, w, bd_bf16):
    ms = _split_dot_rhs(o * o, bd_bf16) * (1.0 / HEAD_DIM)
    return o * lax.rsqrt(ms + NORM_EPS) * w


def _mixer_rows(layer, r0,
                x_ref, n1w_ref, win_ref, cw_ref, cb_ref, wa_ref, ba_ref, wx_ref, bx_ref, lam_ref,
                rnw_ref, lb_ref, hnw_ref, wout_ref, cos_ref, sin_ref,
                pfx_ref, lvl_ref, bd_ref, rint_ref, rdec_ref,
                out_ref,
                lxbuf, lxtail, hcarry, ret_state, hg_state, o_ret, o_hg):
    ts = ROW_BLOCK
    rows_blk = slice(r0, r0 + ts)
    x = x_ref[0, rows_blk, :]
    xn = _rmsnorm(x, n1w_ref[...]).astype(BF16)
    bd = bd_ref[...]
    bd_bf16 = bd.astype(BF16)
    bd_pair = bd[0:PAIR_WIDTH, 0:PAIR_WIDTH]
    n_chunks = ts // CHUNK

    lx = _dot(xn, win_ref[:, 0:LRU_WIDTH])
    pr = _dot(xn, win_ref[:, 2 * LRU_WIDTH:2 * LRU_WIDTH + 4 * GROUP_WIDTH])
    yield

    cw = [cw_ref[k:k + 1, :] for k in range(LRU_CONV)]
    xc = _causal_conv(lx, cw, cb_ref[...], lxbuf, lxtail[...])
    lxtail[...] = lx[ts - SUBLANES:ts, :]
    xcb = xc.astype(BF16)
    half = LRU_WIDTH // 2
    halves = [slice(0, half), slice(half, LRU_WIDTH)]
    r_pre = jnp.concatenate([_dot(xcb[:, hs], wa_ref[hs, hs]) for hs in halves], axis=1)
    i_pre = jnp.concatenate([_dot(xcb[:, hs], wx_ref[hs, hs]) for hs in halves], axis=1)
    lg = _dot(xn, win_ref[:, LRU_WIDTH:2 * LRU_WIDTH])
    ph = _dot(xn, win_ref[:, 2 * LRU_WIDTH + 4 * GROUP_WIDTH:IN_COLS])
    yield

    r_gate = _sigmoid(r_pre + ba_ref[...])
    i_gate = _sigmoid(i_pre + bx_ref[...])
    z = -lam_ref[...]
    softplus = jnp.maximum(z, 0.0) + jnp.log1p(jnp.exp(-jnp.abs(z)))
    av = jnp.exp(-LRU_C * r_gate * softplus)
    uv = jnp.sqrt(1.0 - av * av) * (i_gate * xc)
    row_in_group = lax.broadcasted_iota(jnp.int32, (ts, LRU_WIDTH), 0) % SUBLANES
    for d in (1, 2, 4):
        keep = row_in_group >= d
        a_sh = pltpu.roll(av, d, 0)
        u_sh = pltpu.roll(uv, d, 0)
        uv = jnp.where(keep, av * u_sh + uv, uv)
        av = jnp.where(keep, av * a_sh, av)
    h_prev = hcarry[0:1, :]
    h_groups = []
    for g in range(ts // SUBLANES):
        rows = slice(g * SUBLANES, (g + 1) * SUBLANES)
        h_g = av[rows] * h_prev + uv[rows]
        h_groups.append(h_g)
        h_prev = h_g[SUBLANES - 1:SUBLANES, :]
    hcarry[0:1, :] = h_prev
    y_lru = (jnp.concatenate(h_groups, axis=0) * _gelu_tanh(lg)).astype(BF16)

    cos_t = cos_ref[rows_blk, :]
    sin_t = sin_ref[rows_blk, :]
    rq = pr[:, 0:GROUP_WIDTH]
    rk = pr[:, GROUP_WIDTH:2 * GROUP_WIDTH]
    rv = pr[:, 2 * GROUP_WIDTH:3 * GROUP_WIDTH]
    rg = pr[:, 3 * GROUP_WIDTH:4 * GROUP_WIDTH]
    q_rot = rq * cos_t + _swap_halves(rq) * sin_t
    k_rot = (rk * cos_t + _swap_halves(rk) * sin_t) * (HEAD_DIM ** -0.5)
    q_dec = rdec_ref[0:CHUNK, :]
    k_dec = rdec_ref[CHUNK:2 * CHUNK, :]
    chunk_dec = rdec_ref[2 * CHUNK:2 * CHUNK + 1, :]
    intra = rint_ref[...]
    ret_chunks = []
    for c in range(n_chunks):
        rows = slice(c * CHUNK, (c + 1) * CHUNK)
        qc = q_rot[rows]
        kc = k_rot[rows]
        ret_chunks.append(([(qc.astype(BF16), kc.astype(BF16), intra)], rv[rows], qc * q_dec, kc * k_dec,
                           chunk_dec))
    ret_scores = _attn_scores(ret_chunks)
    yield

    hq = _silu(ph[:, 0:GROUP_WIDTH])
    hf = ph[:, GROUP_WIDTH:2 * GROUP_WIDTH]
    hv = ph[:, 2 * GROUP_WIDTH:3 * GROUP_WIDTH]
    hgate = ph[:, 3 * GROUP_WIDTH:4 * GROUP_WIDTH]
    lbp = lb_ref[...]
    depth = lbp.shape[0]
    mx = lbp[0:1, :]
    for i in range(1, depth):
        mx = jnp.maximum(mx, lbp[i:i + 1, :])
    e = [jnp.exp(lbp[i:i + 1, :] - mx) for i in range(depth)]
    den = functools.reduce(lambda p, q_: p + q_, e)
    lb = jnp.zeros((1, GROUP_WIDTH), F32)
    for i in range(1, layer + 1):
        lb = lb + e[i] / den
    log_sig = jnp.minimum(hf, 0.0) - jnp.log1p(jnp.exp(-jnp.abs(hf)))
    t_a = jnp.log(lb)
    t_b = jnp.log1p(-lb) + log_sig
    t_m = jnp.maximum(t_a, t_b)
    log_f = t_m + jnp.log(jnp.exp(t_a - t_m) + jnp.exp(t_b - t_m))
    hk = (1.0 - lb) * _sigmoid(-hf)
    pfx = pfx_ref[...]
    sums = []
    for c in range(n_chunks):
        gc = log_f[c * CHUNK:(c + 1) * CHUNK]
        g_hi = gc.astype(BF16)
        g_lo = (gc - g_hi.astype(F32)).astype(BF16)
        sums.append(_dot(pfx, jnp.concatenate([g_hi, g_lo], axis=0)))

    ret_pv, ret_upd = _attn_values(ret_chunks, ret_scores, bd_pair)
    yield

    n_lvl = len(LEVELS)
    hg_chunks = []
    for c, ps in enumerate(sums):
        rows = slice(c * CHUNK, (c + 1) * CHUNK)
        b_in = ps[0:CHUNK]
        d_last = ps[CHUNK:2 * CHUNK]
        qc = hq[rows]
        kc = hk[rows]
        levels = []
        for li in range(n_lvl):
            decay = jnp.exp(ps[(2 + li) * CHUNK:(3 + li) * CHUNK])
            levels.append(((qc * decay).astype(BF16), (kc * decay).astype(BF16), lvl_ref[li]))
        levels.append((qc.astype(BF16), kc.astype(BF16), lvl_ref[n_lvl]))
        hg_chunks.append((levels, hv[rows], qc * jnp.exp(b_in), kc * jnp.exp(d_last),
                          jnp.exp(b_in[CHUNK - 1:CHUNK, :])))
    hg_scores = _attn_scores(hg_chunks)

    _attn_finish(ret_chunks, ret_pv, ret_upd, ret_state, o_ret.at[rows_blk])
    yield

    acc = x + _dot(y_lru, wout_ref[0:LRU_WIDTH, :])
    hg_pv, hg_upd = _attn_values(hg_chunks, hg_scores, bd_pair)
    y_ret = (_head_rmsnorm(o_ret[rows_blk, :], rnw_ref[...], bd_bf16) * _silu(rg)).astype(BF16)
    acc = acc + _dot(y_ret, wout_ref[LRU_WIDTH:LRU_WIDTH + GROUP_WIDTH, :])
    yield

    _attn_finish(hg_chunks, hg_pv, hg_upd, hg_state, o_hg.at[rows_blk])
    y_hg = (_head_rmsnorm(o_hg[rows_blk, :], hnw_ref[...], bd_bf16) * _silu(hgate)).astype(BF16)
    acc = acc + _dot(y_hg, wout_ref[LRU_WIDTH + GROUP_WIDTH:D_MODEL, :])
    out_ref[0, rows_blk, :] = acc


def _mixer_kernel(layer, *refs):
    lxtail, hcarry, ret_state, hg_state = refs[-6:-2]

    @pl.when(pl.program_id(1) == 0)
    def _():
        lxtail[...] = jnp.zeros_like(lxtail)
        hcarry[...] = jnp.zeros_like(hcarry)
        ret_state[...] = jnp.zeros_like(ret_state)
        hg_state[...] = jnp.zeros_like(hg_state)

    blocks = [_mixer_rows(layer, r0, *refs) for r0 in range(0, SEQ_TILE, ROW_BLOCK)]
    step = 0
    while any(blk is not None for blk in blocks):
        for bi, blk in enumerate(blocks):
            if bi <= step and blk is not None:
                try:
                    next(blk)
                except StopIteration:
                    blocks[bi] = None
        step += 1


def _const_spec(shape):
    nd = len(shape)
    return pl.BlockSpec(shape, lambda b, s, _nd=nd: (0,) * _nd, pipeline_mode=pl.Buffered(1))


def _mixer_call(layer, x, params, tables, consts):
    bsz, seq, d = x.shape
    ts = SEQ_TILE
    grid = (bsz, seq // ts)
    tile_spec = pl.BlockSpec((1, ts, d), lambda b, s: (b, s, 0))
    tab_spec = pl.BlockSpec((ts, GROUP_WIDTH), lambda b, s: (s, 0))
    weights = [params[k] for k in ("n1w", "win", "cw", "cb", "wa", "ba", "wx", "bx", "lam",
                                   "rnw", "lb", "hnw", "wout")]
    cvals = [consts[k] for k in ("prefix", "level_masks", "block_diag", "ret_intra", "ret_dec")]
    in_specs = ([tile_spec] + [_const_spec(w.shape) for w in weights] + [tab_spec, tab_spec]
                + [_const_spec(c.shape) for c in cvals])
    scratch = [
        pltpu.VMEM((2 * SUBLANES, LRU_WIDTH), F32),
        pltpu.VMEM((SUBLANES, LRU_WIDTH), F32),
        pltpu.VMEM((SUBLANES, LRU_WIDTH), F32),
        pltpu.VMEM((HEADS // PAIR, PAIR_WIDTH, PAIR_WIDTH), F32),
        pltpu.VMEM((HEADS // PAIR, PAIR_WIDTH, PAIR_WIDTH), F32),
        pltpu.VMEM((ts, GROUP_WIDTH), F32),
        pltpu.VMEM((ts, GROUP_WIDTH), F32),
    ]
    return pl.pallas_call(
        functools.partial(_mixer_kernel, layer),
        grid=grid,
        in_specs=in_specs,
        out_specs=tile_spec,
        out_shape=jax.ShapeDtypeStruct(x.shape, F32),
        scratch_shapes=scratch,
        compiler_params=pltpu.CompilerParams(
            dimension_semantics=("arbitrary", "arbitrary"),
            vmem_limit_bytes=VMEM_LIMIT_BYTES),
        name=f"mixer_l{layer}",
    )(x, *weights, tables[0], tables[1], *cvals)


def _ffn_kernel(final, x_ref, n2w_ref, wup_ref, cw_ref, cb_ref, wdown_ref, fnw_ref, out_ref,
                ubuf, ucarry, act_s):
    ts = FFN_SEQ_TILE
    s_idx = pl.program_id(1)

    @pl.when(s_idx == 0)
    def _():
        ucarry[...] = jnp.zeros_like(ucarry)

    x = x_ref[0]
    h = _rmsnorm(x, n2w_ref[...]).astype(BF16)

    def conv_cols(c0, width):
        u = _dot(h, wup_ref[:, c0:c0 + width])
        cw = [cw_ref[k:k + 1, c0:c0 + width] for k in range(FFN_CONV)]
        y = _causal_conv(u, cw, cb_ref[:, c0:c0 + width], ubuf, ucarry[:, c0:c0 + width])
        ucarry[:, c0:c0 + width] = u[ts - SUBLANES:ts, :]
        return y

    c0 = 0
    while c0 < D_FF:
        width = min(FFN_COLS, D_FF - c0)
        gate = conv_cols(c0, width)
        val = conv_cols(D_FF + c0, width)
        act_s[:, c0:c0 + width] = (_silu(gate) * val).astype(BF16)
        c0 += width
    y = x + _dot(act_s[...], wdown_ref[...])
    if final:
        y = _rmsnorm(y, fnw_ref[...])
    out_ref[0] = y


def _ffn_call(final, x, n2w, wup, cw, cb, wdown, fnw):
    bsz, seq, d = x.shape
    ts = FFN_SEQ_TILE
    grid = (bsz, seq // ts)
    tile_spec = pl.BlockSpec((1, ts, d), lambda b, s: (b, s, 0))
    weights = [n2w, wup, cw, cb, wdown, fnw]
    scratch = [
        pltpu.VMEM((2 * SUBLANES, FFN_COLS), F32),
        pltpu.VMEM((SUBLANES, 2 * D_FF), F32),
        pltpu.VMEM((ts, D_FF), BF16),
    ]
    return pl.pallas_call(
        functools.partial(_ffn_kernel, final),
        grid=grid,
        in_specs=[tile_spec] + [_const_spec(w.shape) for w in weights],
        out_specs=tile_spec,
        out_shape=jax.ShapeDtypeStruct(x.shape, F32),
        scratch_shapes=scratch,
        compiler_params=pltpu.CompilerParams(
            dimension_semantics=("arbitrary", "arbitrary"),
            vmem_limit_bytes=VMEM_LIMIT_BYTES),
        name="ffn_final" if final else "ffn",
    )(x, *weights)


def _block_diag(w):
    nb, n, _ = w.shape
    eye = jnp.eye(nb, dtype=w.dtype)
    return (eye[:, None, :, None] * w[:, :, None, :]).reshape(nb * n, nb * n)


def kernel(x, norm1_w, w_in, lru_conv_w, lru_conv_b, lru_wa, lru_ba, lru_wx, lru_bx, lru_lambda,
           ret_norm_w, hg_lower_bounds, hg_norm_w, w_out, norm2_w, ffn_w_up, ffn_conv_w, ffn_conv_b,
           ffn_w_down, final_norm_w):
    bsz, seq, d = x.shape
    depth = w_in.shape[0]
    assert d == D_MODEL and seq % SEQ_TILE == 0 and seq % FFN_SEQ_TILE == 0
    consts = _constants()
    tables = _rotary_tables(seq)
    row = lambda v: v.reshape(1, -1).astype(F32)
    x = x.astype(F32)
    for l in range(depth):
        params = dict(
            n1w=row(norm1_w[l]), win=w_in[l].astype(BF16), cw=lru_conv_w[l].astype(F32),
            cb=row(lru_conv_b[l]), wa=_block_diag(lru_wa[l]).astype(BF16), ba=row(lru_ba[l]),
            wx=_block_diag(lru_wx[l]).astype(BF16), bx=row(lru_bx[l]), lam=row(lru_lambda[l]),
            rnw=row(ret_norm_w[l]), lb=hg_lower_bounds.astype(F32), hnw=row(hg_norm_w[l]),
            wout=w_out[l].astype(BF16))
        x = _mixer_call(l, x, params, tables, consts)
        x = _ffn_call(l == depth - 1, x, row(norm2_w[l]), ffn_w_up[l].astype(BF16),
                      ffn_conv_w[l].astype(F32), row(ffn_conv_b[l]), ffn_w_down[l].astype(BF16),
                      row(final_norm_w))
    return x
```

```python
import functools

import jax
import jax.numpy as jnp
import numpy as np
from jax import lax
from jax.experimental import pallas as pl
from jax.experimental.pallas import tpu as pltpu

F32 = jnp.float32
BF16 = jnp.bfloat16

D_MODEL = 1024
LRU_WIDTH = 512
LRU_BLOCKS = 8
LRU_CONV = 4
LRU_C = 8.0
HEADS = 4
HEAD_DIM = 64
GROUP_WIDTH = HEADS * HEAD_DIM
ROPE_BASE = 10000.0
IN_COLS = 2 * LRU_WIDTH + 8 * GROUP_WIDTH
D_FF = 2816
FFN_CONV = 3
NORM_EPS = 1e-6

SUBLANES = 8
SEQ_TILE = 512
CHUNK = 64
LEVELS = (1, 2, 4, 8, 16, 32)
PAIR = 2
PAIR_WIDTH = PAIR * HEAD_DIM
FFN_COLS = 512
VMEM_LIMIT_BYTES = 56 * 1024 * 1024


def _dot(a, b):
    return jnp.dot(a, b, preferred_element_type=F32)


def _dot_nt(a, b):
    return lax.dot_general(a, b, (((1,), (1,)), ((), ())), preferred_element_type=F32)


def _dot_tn(a, b):
    return lax.dot_general(a, b, (((0,), (0,)), ((), ())), preferred_element_type=F32)


def _rmsnorm(x, w):
    ms = jnp.mean(x * x, axis=-1, keepdims=True)
    return x * lax.rsqrt(ms + NORM_EPS) * w


def _sigmoid(x):
    return 0.5 * jnp.tanh(0.5 * x) + 0.5


def _silu(x):
    return x * _sigmoid(x)


def _gelu_tanh(x):
    return 0.5 * x * (1.0 + jnp.tanh(np.sqrt(2.0 / np.pi) * (x + 0.044715 * (x * x * x))))


def _split_dot_rhs(x, const_bf16):
    hi = x.astype(BF16)
    lo = (x - hi.astype(F32)).astype(BF16)
    return _dot(hi, const_bf16) + _dot(lo, const_bf16)


def _attention_tile(chunks, state_ref, bd_pair, out_ref):
    c = CHUNK
    pairs = range(HEADS // PAIR)
    lanes = [slice(p * PAIR_WIDTH, (p + 1) * PAIR_WIDTH) for p in pairs]
    low = lax.broadcasted_iota(jnp.int32, (c, PAIR_WIDTH), 1) < HEAD_DIM

    scores = []
    for levels, _, _, _, _ in chunks:
        per_pair = []
        for p in pairs:
            sc = None
            for ql, kl, mask in levels:
                qp = ql[:, lanes[p]]
                zero = jnp.zeros_like(qp)
                lhs = jnp.concatenate([jnp.where(low, qp, zero), jnp.where(low, zero, qp)], axis=0)
                t = (mask if mask.ndim == 2 else mask[p]) * _dot_nt(lhs, kl[:, lanes[p]])
                sc = t if sc is None else sc + t
            per_pair.append(sc.astype(BF16))
        scores.append(per_pair)

    pv, upd = [], []
    for (_, vc, _, k_out, _), per_pair in zip(chunks, scores):
        vps = [vc[:, lanes[p]].astype(BF16) for p in pairs]
        pv.append([_dot(per_pair[p], vps[p]) for p in pairs])
        upd.append([bd_pair * _dot_tn(vps[p], k_out[:, lanes[p]].astype(BF16)) for p in pairs])

    outs = [[None] * len(lanes) for _ in chunks]
    for p in pairs:
        st = state_ref[p]
        for ci, (_, _, q_in, _, dec) in enumerate(chunks):
            o2 = pv[ci][p]
            outs[ci][p] = (jnp.where(low, o2[0:c], o2[c:2 * c])
                           + _dot_nt(q_in[:, lanes[p]].astype(BF16), st.astype(BF16)))
            st = st * dec[:, lanes[p]] + upd[ci][p]
        state_ref[p] = st
    for ci in range(len(chunks)):
        out_ref[ci * c:(ci + 1) * c, :] = jnp.concatenate(outs[ci], axis=1)


def _causal_conv(x, w_rows, b_row, head_buf, tail):
    taps = len(w_rows)
    ts, width = x.shape
    y = b_row + w_rows[taps - 1] * x
    for k in range(taps - 1):
        y = y + w_rows[k] * pltpu.roll(x, taps - 1 - k, 0)
    head_buf[0:SUBLANES, 0:width] = tail
    head_buf[SUBLANES:2 * SUBLANES, 0:width] = x[0:SUBLANES, :]
    yh = b_row + w_rows[taps - 1] * x[0:SUBLANES, :]
    for k in range(taps - 1):
        back = taps - 1 - k
        yh = yh + w_rows[k] * head_buf[SUBLANES - back:2 * SUBLANES - back, 0:width]
    return jnp.concatenate([yh, y[SUBLANES:ts, :]], axis=0)


def _constants():
    c = CHUNK
    n = np.arange(c)[:, None]
    j = np.arange(c)[None, :]
    blocks = [j <= n, j > n]
    masks = []
    n2 = np.arange(PAIR * c)[:, None] % c
    for s in LEVELS:
        start = (n // (2 * s)) * (2 * s)
        anchor = start + s - 1
        second = (n - start) >= s
        blocks.append((second & (j > anchor) & (j <= n)) | ((~second) & (j > n) & (j <= anchor)))
        same = (n2 // (2 * s)) == (j // (2 * s))
        masks.append(same & ((n2 % (2 * s)) >= s) & ((j % (2 * s)) < s))
    masks.append(n2 == j)
    prefix = np.concatenate(blocks, axis=0).astype(np.float32)
    prefix2 = np.concatenate([prefix, prefix], axis=1)
    level_masks = np.stack(masks, axis=0).astype(np.float32)
    r = np.arange(GROUP_WIDTH)
    block_diag = ((r[:, None] // HEAD_DIM) == (r[None, :] // HEAD_DIM)).astype(np.float32)

    log_gamma = np.log1p(-np.exp2(-5.0 - np.arange(HEADS, dtype=np.float64)))
    lg_lane = log_gamma[r // HEAD_DIM][None, :]
    pos = np.arange(c, dtype=np.float64)[:, None]
    rel = (n2 - j).astype(np.float64)
    intra = []
    for p in range(HEADS // PAIR):
        lg_row = log_gamma[PAIR * p + np.arange(PAIR * c) // c][:, None]
        intra.append(np.where(rel >= 0, np.exp(np.maximum(rel, 0.0) * lg_row), 0.0))
    q_dec = np.exp((pos + 1.0) * lg_lane)
    k_dec = np.exp((c - 1.0 - pos) * lg_lane)
    chunk_dec = np.exp(c * lg_lane)
    ret_dec = np.concatenate([q_dec, k_dec, np.broadcast_to(chunk_dec, (SUBLANES, GROUP_WIDTH))], axis=0)
    return dict(
        prefix=jnp.asarray(prefix2, BF16),
        level_masks=jnp.asarray(level_masks, F32),
        block_diag=jnp.asarray(block_diag, F32),
        ret_intra=jnp.asarray(np.stack(intra, axis=0), F32),
        ret_dec=jnp.asarray(ret_dec, F32),
    )


def _rotary_tables(seq):
    inv = ROPE_BASE ** (-jnp.arange(0, HEAD_DIM, 2, dtype=F32) / HEAD_DIM)
    ang = jnp.arange(seq, dtype=F32)[:, None] * inv[None, :]
    cos = jnp.cos(ang)
    sin = jnp.sin(ang)
    cos_t = jnp.tile(cos, (1, 2 * HEADS))
    sin_t = jnp.tile(jnp.concatenate([-sin, sin], axis=1), (1, HEADS))
    return cos_t, sin_t


def _swap_halves(x):
    w = x.shape[-1]
    fwd = pltpu.roll(x, HEAD_DIM // 2, 1)
    bwd = pltpu.roll(x, w - HEAD_DIM // 2, 1)
    lane = lax.broadcasted_iota(jnp.int32, x.shape, 1)
    return jnp.where((lane % HEAD_DIM) < HEAD_DIM // 2, bwd, fwd)


def _head_rmsnorm(o, w, bd_bf16):
    ms = _split_dot_rhs(o * o, bd_bf16) * (1.0 / HEAD_DIM)
    return o * lax.rsqrt(ms + NORM_EPS) * w


def _mixer_kernel(layer,
                  x_ref, n1w_ref, win_ref, cw_ref, cb_ref, wa_ref, ba_ref, wx_ref, bx_ref, lam_ref,
                  rnw_ref, lb_ref, hnw_ref, wout_ref, cos_ref, sin_ref,
                  pfx_ref, lvl_ref, bd_ref, rint_ref, rdec_ref,
                  out_ref,
                  lxbuf, lxtail, a_s, u_s, h_s, hcarry, ret_state, hg_state, o_s):
    ts = SEQ_TILE
    s_idx = pl.program_id(1)

    @pl.when(s_idx == 0)
    def _():
        lxtail[...] = jnp.zeros_like(lxtail)
        hcarry[...] = jnp.zeros_like(hcarry)
        ret_state[...] = jnp.zeros_like(ret_state)
        hg_state[...] = jnp.zeros_like(hg_state)

    x = x_ref[0]
    xn = _rmsnorm(x, n1w_ref[...]).astype(BF16)
    bd = bd_ref[...]
    bd_bf16 = bd.astype(BF16)

    lx = _dot(xn, win_ref[:, 0:LRU_WIDTH])
    lg = _dot(xn, win_ref[:, LRU_WIDTH:2 * LRU_WIDTH])
    cw = [cw_ref[k:k + 1, :] for k in range(LRU_CONV)]
    xc = _causal_conv(lx, cw, cb_ref[...], lxbuf, lxtail[...])
    lxtail[...] = lx[ts - SUBLANES:ts, :]
    xcb = xc.astype(BF16)
    half = LRU_WIDTH // 2
    halves = [slice(0, half), slice(half, LRU_WIDTH)]
    r_gate = _sigmoid(jnp.concatenate([_dot(xcb[:, hs], wa_ref[hs, hs]) for hs in halves], axis=1) + ba_ref[...])
    i_gate = _sigmoid(jnp.concatenate([_dot(xcb[:, hs], wx_ref[hs, hs]) for hs in halves], axis=1) + bx_ref[...])
    z = -lam_ref[...]
    softplus = jnp.maximum(z, 0.0) + jnp.log1p(jnp.exp(-jnp.abs(z)))
    a = jnp.exp(-LRU_C * r_gate * softplus)
    a_s[...] = a
    u_s[...] = jnp.sqrt(1.0 - a * a) * (i_gate * xc)

    row8 = lax.broadcasted_iota(jnp.int32, (SUBLANES, LRU_WIDTH), 0)

    def scan_body(g, h_prev):
        r0 = pl.multiple_of(g * SUBLANES, SUBLANES)
        av = a_s[pl.ds(r0, SUBLANES), :]
        uv = u_s[pl.ds(r0, SUBLANES), :]
        for d in (1, 2, 4):
            keep = row8 >= d
            a_sh = pltpu.roll(av, d, 0)
            u_sh = pltpu.roll(uv, d, 0)
            uv = jnp.where(keep, av * u_sh + uv, uv)
            av = jnp.where(keep, av * a_sh, av)
        h = av * h_prev + uv
        h_s[pl.ds(r0, SUBLANES), :] = h
        return h[SUBLANES - 1:SUBLANES, :]

    h_last = lax.fori_loop(0, ts // SUBLANES, scan_body, hcarry[0:1, :], unroll=4)
    hcarry[0:1, :] = h_last
    y_lru = (h_s[...] * _gelu_tanh(lg)).astype(BF16)
    acc = x + _dot(y_lru, wout_ref[0:LRU_WIDTH, :])

    pr = _dot(xn, win_ref[:, 2 * LRU_WIDTH:2 * LRU_WIDTH + 4 * GROUP_WIDTH])
    cos_t = cos_ref[...]
    sin_t = sin_ref[...]
    rq = pr[:, 0:GROUP_WIDTH]
    rk = pr[:, GROUP_WIDTH:2 * GROUP_WIDTH]
    rv = pr[:, 2 * GROUP_WIDTH:3 * GROUP_WIDTH]
    rg = pr[:, 3 * GROUP_WIDTH:4 * GROUP_WIDTH]
    q_rot = rq * cos_t + _swap_halves(rq) * sin_t
    k_rot = (rk * cos_t + _swap_halves(rk) * sin_t) * (HEAD_DIM ** -0.5)
    q_dec = rdec_ref[0:CHUNK, :]
    k_dec = rdec_ref[CHUNK:2 * CHUNK, :]
    chunk_dec = rdec_ref[2 * CHUNK:2 * CHUNK + 1, :]
    bd_pair = bd[0:PAIR_WIDTH, 0:PAIR_WIDTH]
    intra = rint_ref[...]
    chunks = []
    for c in range(ts // CHUNK):
        rows = slice(c * CHUNK, (c + 1) * CHUNK)
        qc = q_rot[rows]
        kc = k_rot[rows]
        chunks.append(([(qc.astype(BF16), kc.astype(BF16), intra)], rv[rows], qc * q_dec, kc * k_dec, chunk_dec))
    _attention_tile(chunks, ret_state, bd_pair, o_s)
    y_ret = (_head_rmsnorm(o_s[...], rnw_ref[...], bd_bf16) * _silu(rg)).astype(BF16)
    acc = acc + _dot(y_ret, wout_ref[LRU_WIDTH:LRU_WIDTH + GROUP_WIDTH, :])

    ph = _dot(xn, win_ref[:, 2 * LRU_WIDTH + 4 * GROUP_WIDTH:IN_COLS])
    hq = _silu(ph[:, 0:GROUP_WIDTH])
    hf = ph[:, GROUP_WIDTH:2 * GROUP_WIDTH]
    hv = ph[:, 2 * GROUP_WIDTH:3 * GROUP_WIDTH]
    hgate = ph[:, 3 * GROUP_WIDTH:4 * GROUP_WIDTH]
    lbp = lb_ref[...]
    depth = lbp.shape[0]
    mx = lbp[0:1, :]
    for i in range(1, depth):
        mx = jnp.maximum(mx, lbp[i:i + 1, :])
    e = [jnp.exp(lbp[i:i + 1, :] - mx) for i in range(depth)]
    den = functools.reduce(lambda p, q_: p + q_, e)
    lb = jnp.zeros((1, GROUP_WIDTH), F32)
    for i in range(1, layer + 1):
        lb = lb + e[i] / den
    log_sig = jnp.minimum(hf, 0.0) - jnp.log1p(jnp.exp(-jnp.abs(hf)))
    t_a = jnp.log(lb)
    t_b = jnp.log1p(-lb) + log_sig
    t_m = jnp.maximum(t_a, t_b)
    log_f = t_m + jnp.log(jnp.exp(t_a - t_m) + jnp.exp(t_b - t_m))
    hk = (1.0 - lb) * _sigmoid(-hf)

    pfx = pfx_ref[...]
    n_lvl = len(LEVELS)
    sums = []
    for c in range(ts // CHUNK):
        gc = log_f[c * CHUNK:(c + 1) * CHUNK]
        g_hi = gc.astype(BF16)
        g_lo = (gc - g_hi.astype(F32)).astype(BF16)
        sums.append(_dot(pfx, jnp.concatenate([g_hi, g_lo], axis=0)))
    chunks = []
    for c, ps in enumerate(sums):
        rows = slice(c * CHUNK, (c + 1) * CHUNK)
        b_in = ps[0:CHUNK]
        d_last = ps[CHUNK:2 * CHUNK]
        qc = hq[rows]
        kc = hk[rows]
        levels = []
        for li in range(n_lvl):
            decay = jnp.exp(ps[(2 + li) * CHUNK:(3 + li) * CHUNK])
            levels.append(((qc * decay).astype(BF16), (kc * decay).astype(BF16), lvl_ref[li]))
        levels.append((qc.astype(BF16), kc.astype(BF16), lvl_ref[n_lvl]))
        chunks.append((levels, hv[rows], qc * jnp.exp(b_in), kc * jnp.exp(d_last),
                       jnp.exp(b_in[CHUNK - 1:CHUNK, :])))
    _attention_tile(chunks, hg_state, bd_pair, o_s)
    y_hg = (_head_rmsnorm(o_s[...], hnw_ref[...], bd_bf16) * _silu(hgate)).astype(BF16)
    acc = acc + _dot(y_hg, wout_ref[LRU_WIDTH + GROUP_WIDTH:D_MODEL, :])
    out_ref[0] = acc


def _const_spec(shape):
    nd = len(shape)
    return pl.BlockSpec(shape, lambda b, s, _nd=nd: (0,) * _nd, pipeline_mode=pl.Buffered(1))


def _mixer_call(layer, x, params, tables, consts):
    bsz, seq, d = x.shape
    ts = SEQ_TILE
    grid = (bsz, seq // ts)
    tile_spec = pl.BlockSpec((1, ts, d), lambda b, s: (b, s, 0))
    tab_spec = pl.BlockSpec((ts, GROUP_WIDTH), lambda b, s: (s, 0))
    weights = [params[k] for k in ("n1w", "win", "cw", "cb", "wa", "ba", "wx", "bx", "lam",
                                   "rnw", "lb", "hnw", "wout")]
    cvals = [consts[k] for k in ("prefix", "level_masks", "block_diag", "ret_intra", "ret_dec")]
    in_specs = ([tile_spec] + [_const_spec(w.shape) for w in weights] + [tab_spec, tab_spec]
                + [_const_spec(c.shape) for c in cvals])
    scratch = [
        pltpu.VMEM((2 * SUBLANES, LRU_WIDTH), F32),
        pltpu.VMEM((SUBLANES, LRU_WIDTH), F32),
        pltpu.VMEM((ts, LRU_WIDTH), F32),
        pltpu.VMEM((ts, LRU_WIDTH), F32),
        pltpu.VMEM((ts, LRU_WIDTH), F32),
        pltpu.VMEM((SUBLANES, LRU_WIDTH), F32),
        pltpu.VMEM((HEADS // PAIR, PAIR_WIDTH, PAIR_WIDTH), F32),
        pltpu.VMEM((HEADS // PAIR, PAIR_WIDTH, PAIR_WIDTH), F32),
        pltpu.VMEM((ts, GROUP_WIDTH), F32),
    ]
    return pl.pallas_call(
        functools.partial(_mixer_kernel, layer),
        grid=grid,
        in_specs=in_specs,
        out_specs=tile_spec,
        out_shape=jax.ShapeDtypeStruct(x.shape, F32),
        scratch_shapes=scratch,
        compiler_params=pltpu.CompilerParams(
            dimension_semantics=("arbitrary", "arbitrary"),
            vmem_limit_bytes=VMEM_LIMIT_BYTES),
        name=f"mixer_l{layer}",
    )(x, *weights, tables[0], tables[1], *cvals)


def _ffn_kernel(final, x_ref, n2w_ref, wup_ref, cw_ref, cb_ref, wdown_ref, fnw_ref, out_ref,
                ubuf, ucarry, act_s):
    ts = SEQ_TILE
    s_idx = pl.program_id(1)

    @pl.when(s_idx == 0)
    def _():
        ucarry[...] = jnp.zeros_like(ucarry)

    x = x_ref[0]
    h = _rmsnorm(x, n2w_ref[...]).astype(BF16)

    def conv_cols(c0, width):
        u = _dot(h, wup_ref[:, c0:c0 + width])
        cw = [cw_ref[k:k + 1, c0:c0 + width] for k in range(FFN_CONV)]
        y = _causal_conv(u, cw, cb_ref[:, c0:c0 + width], ubuf, ucarry[:, c0:c0 + width])
        ucarry[:, c0:c0 + width] = u[ts - SUBLANES:ts, :]
        return y

    c0 = 0
    while c0 < D_FF:
        width = min(FFN_COLS, D_FF - c0)
        gate = conv_cols(c0, width)
        val = conv_cols(D_FF + c0, width)
        act_s[:, c0:c0 + width] = (_silu(gate) * val).astype(BF16)
        c0 += width
    y = x + _dot(act_s[...], wdown_ref[...])
    if final:
        y = _rmsnorm(y, fnw_ref[...])
    out_ref[0] = y


def _ffn_call(final, x, n2w, wup, cw, cb, wdown, fnw):
    bsz, seq, d = x.shape
    ts = SEQ_TILE
    grid = (bsz, seq // ts)
    tile_spec = pl.BlockSpec((1, ts, d), lambda b, s: (b, s, 0))
    weights = [n2w, wup, cw, cb, wdown, fnw]
    scratch = [
        pltpu.VMEM((2 * SUBLANES, FFN_COLS), F32),
        pltpu.VMEM((SUBLANES, 2 * D_FF), F32),
        pltpu.VMEM((ts, D_FF), BF16),
    ]
    return pl.pallas_call(
        functools.partial(_ffn_kernel, final),
        grid=grid,
        in_specs=[tile_spec] + [_const_spec(w.shape) for w in weights],
        out_specs=tile_spec,
        out_shape=jax.ShapeDtypeStruct(x.shape, F32),
        scratch_shapes=scratch,
        compiler_params=pltpu.CompilerParams(
            dimension_semantics=("arbitrary", "arbitrary"),
            vmem_limit_bytes=VMEM_LIMIT_BYTES),
        name="ffn_final" if final else "ffn",
    )(x, *weights)


def _block_diag(w):
    nb, n, _ = w.shape
    eye = jnp.eye(nb, dtype=w.dtype)
    return (eye[:, None, :, None] * w[:, :, None, :]).reshape(nb * n, nb * n)


def kernel(x, norm1_w, w_in, lru_conv_w, lru_conv_b, lru_wa, lru_ba, lru_wx, lru_bx, lru_lambda,
           ret_norm_w, hg_lower_bounds, hg_norm_w, w_out, norm2_w, ffn_w_up, ffn_conv_w, ffn_conv_b,
           ffn_w_down, final_norm_w):
    bsz, seq, d = x.shape
    depth = w_in.shape[0]
    assert d == D_MODEL and seq % SEQ_TILE == 0
    consts = _constants()
    tables = _rotary_tables(seq)
    row = lambda v: v.reshape(1, -1).astype(F32)
    x = x.astype(F32)
    for l in range(depth):
        params = dict(
            n1w=row(norm1_w[l]), win=w_in[l].astype(BF16), cw=lru_conv_w[l].astype(F32),
            cb=row(lru_conv_b[l]), wa=_block_diag(lru_wa[l]).astype(BF16), ba=row(lru_ba[l]),
            wx=_block_diag(lru_wx[l]).astype(BF16), bx=row(lru_bx[l]), lam=row(lru_lambda[l]),
            rnw=row(ret_norm_w[l]), lb=hg_lower_bounds.astype(F32), hnw=row(hg_norm_w[l]),
            wout=w_out[l].astype(BF16))
        x = _mixer_call(l, x, params, tables, consts)
        x = _ffn_call(l == depth - 1, x, row(norm2_w[l]), ffn_w_up[l].astype(BF16),
                      ffn_conv_w[l].astype(F32), row(ffn_conv_b[l]), ffn_w_down[l].astype(BF16),
                      row(final_norm_w))
    return x
```

```python
import functools

import jax
import jax.numpy as jnp
import numpy as np
from jax import lax
from jax.experimental import pallas as pl
from jax.experimental.pallas import tpu as pltpu

F32 = jnp.float32
BF16 = jnp.bfloat16

D_MODEL = 1024
LRU_WIDTH = 512
LRU_BLOCKS = 8
LRU_CONV = 4
LRU_C = 8.0
HEADS = 4
HEAD_DIM = 64
GROUP_WIDTH = HEADS * HEAD_DIM
ROPE_BASE = 10000.0
IN_COLS = 2 * LRU_WIDTH + 8 * GROUP_WIDTH
D_FF = 2816
FFN_CONV = 3
NORM_EPS = 1e-6

SUBLANES = 8
SEQ_TILE = 512
CHUNK = 64
LEVELS = (1, 2, 4, 8, 16, 32)
PAIR = 2
PAIR_WIDTH = PAIR * HEAD_DIM
FFN_COLS = 512
VMEM_LIMIT_BYTES = 56 * 1024 * 1024


def _dot(a, b):
    return jnp.dot(a, b, preferred_element_type=F32)


def _dot_nt(a, b):
    return lax.dot_general(a, b, (((1,), (1,)), ((), ())), preferred_element_type=F32)


def _dot_tn(a, b):
    return lax.dot_general(a, b, (((0,), (0,)), ((), ())), preferred_element_type=F32)


def _rmsnorm(x, w):
    ms = jnp.mean(x * x, axis=-1, keepdims=True)
    return x * lax.rsqrt(ms + NORM_EPS) * w


def _sigmoid(x):
    return 0.5 * jnp.tanh(0.5 * x) + 0.5


def _silu(x):
    return x * _sigmoid(x)


def _gelu_tanh(x):
    return 0.5 * x * (1.0 + jnp.tanh(np.sqrt(2.0 / np.pi) * (x + 0.044715 * (x * x * x))))


def _split_dot_rhs(x, const_bf16):
    hi = x.astype(BF16)
    lo = (x - hi.astype(F32)).astype(BF16)
    return _dot(hi, const_bf16) + _dot(lo, const_bf16)


def _attention_tile(chunks, state_ref, bd_pair, out_ref):
    c = CHUNK
    pairs = range(HEADS // PAIR)
    lanes = [slice(p * PAIR_WIDTH, (p + 1) * PAIR_WIDTH) for p in pairs]
    low = lax.broadcasted_iota(jnp.int32, (c, PAIR_WIDTH), 1) < HEAD_DIM

    scores = []
    for levels, _, _, _, _ in chunks:
        per_pair = []
        for p in pairs:
            sc = None
            for ql, kl, mask in levels:
                qp = ql[:, lanes[p]]
                zero = jnp.zeros_like(qp)
                lhs = jnp.concatenate([jnp.where(low, qp, zero), jnp.where(low, zero, qp)], axis=0)
                t = (mask if mask.ndim == 2 else mask[p]) * _dot_nt(lhs, kl[:, lanes[p]])
                sc = t if sc is None else sc + t
            per_pair.append(sc.astype(BF16))
        scores.append(per_pair)

    pv, upd = [], []
    for (_, vc, _, k_out, _), per_pair in zip(chunks, scores):
        vps = [vc[:, lanes[p]].astype(BF16) for p in pairs]
        pv.append([_dot(per_pair[p], vps[p]) for p in pairs])
        upd.append([bd_pair * _dot_tn(vps[p], k_out[:, lanes[p]].astype(BF16)) for p in pairs])

    outs = [[None] * len(lanes) for _ in chunks]
    for p in pairs:
        st = state_ref[p]
        for ci, (_, _, q_in, _, dec) in enumerate(chunks):
            o2 = pv[ci][p]
            outs[ci][p] = (jnp.where(low, o2[0:c], o2[c:2 * c])
                           + _dot_nt(q_in[:, lanes[p]].astype(BF16), st.astype(BF16)))
            st = st * dec[:, lanes[p]] + upd[ci][p]
        state_ref[p] = st
    for ci in range(len(chunks)):
        out_ref[ci * c:(ci + 1) * c, :] = jnp.concatenate(outs[ci], axis=1)


def _causal_conv(x, w_rows, b_row, head_buf, tail):
    taps = len(w_rows)
    ts, width = x.shape
    y = b_row + w_rows[taps - 1] * x
    for k in range(taps - 1):
        y = y + w_rows[k] * pltpu.roll(x, taps - 1 - k, 0)
    head_buf[0:SUBLANES, 0:width] = tail
    head_buf[SUBLANES:2 * SUBLANES, 0:width] = x[0:SUBLANES, :]
    yh = b_row + w_rows[taps - 1] * x[0:SUBLANES, :]
    for k in range(taps - 1):
        back = taps - 1 - k
        yh = yh + w_rows[k] * head_buf[SUBLANES - back:2 * SUBLANES - back, 0:width]
    return jnp.concatenate([yh, y[SUBLANES:ts, :]], axis=0)


def _constants():
    c = CHUNK
    n = np.arange(c)[:, None]
    j = np.arange(c)[None, :]
    blocks = [j <= n, j > n]
    masks = []
    n2 = np.arange(PAIR * c)[:, None] % c
    for s in LEVELS:
        start = (n // (2 * s)) * (2 * s)
        anchor = start + s - 1
        second = (n - start) >= s
        blocks.append((second & (j > anchor) & (j <= n)) | ((~second) & (j > n) & (j <= anchor)))
        same = (n2 // (2 * s)) == (j // (2 * s))
        masks.append(same & ((n2 % (2 * s)) >= s) & ((j % (2 * s)) < s))
    masks.append(n2 == j)
    prefix = np.concatenate(blocks, axis=0).astype(np.float32)
    prefix2 = np.concatenate([prefix, prefix], axis=1)
    level_masks = np.stack(masks, axis=0).astype(np.float32)
    r = np.arange(GROUP_WIDTH)
    block_diag = ((r[:, None] // HEAD_DIM) == (r[None, :] // HEAD_DIM)).astype(np.float32)

    log_gamma = np.log1p(-np.exp2(-5.0 - np.arange(HEADS, dtype=np.float64)))
    lg_lane = log_gamma[r // HEAD_DIM][None, :]
    pos = np.arange(c, dtype=np.float64)[:, None]
    rel = (n2 - j).astype(np.float64)
    intra = []
    for p in range(HEADS // PAIR):
        lg_row = log_gamma[PAIR * p + np.arange(PAIR * c) // c][:, None]
        intra.append(np.where(rel >= 0, np.exp(np.maximum(rel, 0.0) * lg_row), 0.0))
    q_dec = np.exp((pos + 1.0) * lg_lane)
    k_dec = np.exp((c - 1.0 - pos) * lg_lane)
    chunk_dec = np.exp(c * lg_lane)
    ret_dec = np.concatenate([q_dec, k_dec, np.broadcast_to(chunk_dec, (SUBLANES, GROUP_WIDTH))], axis=0)
    return dict(
        prefix=jnp.asarray(prefix2, BF16),
        level_masks=jnp.asarray(level_masks, F32),
        block_diag=jnp.asarray(block_diag, F32),
        ret_intra=jnp.asarray(np.stack(intra, axis=0), F32),
        ret_dec=jnp.asarray(ret_dec, F32),
    )


def _rotary_tables(seq):
    inv = ROPE_BASE ** (-jnp.arange(0, HEAD_DIM, 2, dtype=F32) / HEAD_DIM)
    ang = jnp.arange(seq, dtype=F32)[:, None] * inv[None, :]
    cos = jnp.cos(ang)
    sin = jnp.sin(ang)
    cos_t = jnp.tile(cos, (1, 2 * HEADS))
    sin_t = jnp.tile(jnp.concatenate([-sin, sin], axis=1), (1, HEADS))
    return cos_t, sin_t


def _swap_halves(x):
    w = x.shape[-1]
    fwd = pltpu.roll(x, HEAD_DIM // 2, 1)
    bwd = pltpu.roll(x, w - HEAD_DIM // 2, 1)
    lane = lax.broadcasted_iota(jnp.int32, x.shape, 1)
    return jnp.where((lane % HEAD_DIM) < HEAD_DIM // 2, bwd, fwd)


def _head_rmsnorm(o, w, bd_bf16):
    ms = _split_dot_rhs(o * o, bd_bf16) * (1.0 / HEAD_DIM)
    return o * lax.rsqrt(ms + NORM_EPS) * w


def _mixer_kernel(layer,
                  x_ref, n1w_ref, win_ref, cw_ref, cb_ref, wa_ref, ba_ref, wx_ref, bx_ref, lam_ref,
                  rnw_ref, lb_ref, hnw_ref, wout_ref, cos_ref, sin_ref,
                  pfx_ref, lvl_ref, bd_ref, rint_ref, rdec_ref,
                  out_ref,
                  lxbuf, lxtail, hcarry, ret_state, hg_state, o_s):
    ts = SEQ_TILE
    s_idx = pl.program_id(1)

    @pl.when(s_idx == 0)
    def _():
        lxtail[...] = jnp.zeros_like(lxtail)
        hcarry[...] = jnp.zeros_like(hcarry)
        ret_state[...] = jnp.zeros_like(ret_state)
        hg_state[...] = jnp.zeros_like(hg_state)

    x = x_ref[0]
    xn = _rmsnorm(x, n1w_ref[...]).astype(BF16)
    bd = bd_ref[...]
    bd_bf16 = bd.astype(BF16)

    lx = _dot(xn, win_ref[:, 0:LRU_WIDTH])
    lg = _dot(xn, win_ref[:, LRU_WIDTH:2 * LRU_WIDTH])
    cw = [cw_ref[k:k + 1, :] for k in range(LRU_CONV)]
    xc = _causal_conv(lx, cw, cb_ref[...], lxbuf, lxtail[...])
    lxtail[...] = lx[ts - SUBLANES:ts, :]
    xcb = xc.astype(BF16)
    half = LRU_WIDTH // 2
    halves = [slice(0, half), slice(half, LRU_WIDTH)]
    r_gate = _sigmoid(jnp.concatenate([_dot(xcb[:, hs], wa_ref[hs, hs]) for hs in halves], axis=1) + ba_ref[...])
    i_gate = _sigmoid(jnp.concatenate([_dot(xcb[:, hs], wx_ref[hs, hs]) for hs in halves], axis=1) + bx_ref[...])
    z = -lam_ref[...]
    softplus = jnp.maximum(z, 0.0) + jnp.log1p(jnp.exp(-jnp.abs(z)))
    a = jnp.exp(-LRU_C * r_gate * softplus)
    u = jnp.sqrt(1.0 - a * a) * (i_gate * xc)

    pr = _dot(xn, win_ref[:, 2 * LRU_WIDTH:2 * LRU_WIDTH + 4 * GROUP_WIDTH])
    row8 = lax.broadcasted_iota(jnp.int32, (SUBLANES, LRU_WIDTH), 0)
    h_prev = hcarry[0:1, :]
    h_groups = []
    for g in range(ts // SUBLANES):
        rows = slice(g * SUBLANES, (g + 1) * SUBLANES)
        av = a[rows]
        uv = u[rows]
        for d in (1, 2, 4):
            keep = row8 >= d
            a_sh = pltpu.roll(av, d, 0)
            u_sh = pltpu.roll(uv, d, 0)
            uv = jnp.where(keep, av * u_sh + uv, uv)
            av = jnp.where(keep, av * a_sh, av)
        h = av * h_prev + uv
        h_groups.append(h)
        h_prev = h[SUBLANES - 1:SUBLANES, :]
    hcarry[0:1, :] = h_prev
    y_lru = (jnp.concatenate(h_groups, axis=0) * _gelu_tanh(lg)).astype(BF16)
    acc = x + _dot(y_lru, wout_ref[0:LRU_WIDTH, :])

    cos_t = cos_ref[...]
    sin_t = sin_ref[...]
    rq = pr[:, 0:GROUP_WIDTH]
    rk = pr[:, GROUP_WIDTH:2 * GROUP_WIDTH]
    rv = pr[:, 2 * GROUP_WIDTH:3 * GROUP_WIDTH]
    rg = pr[:, 3 * GROUP_WIDTH:4 * GROUP_WIDTH]
    q_rot = rq * cos_t + _swap_halves(rq) * sin_t
    k_rot = (rk * cos_t + _swap_halves(rk) * sin_t) * (HEAD_DIM ** -0.5)
    q_dec = rdec_ref[0:CHUNK, :]
    k_dec = rdec_ref[CHUNK:2 * CHUNK, :]
    chunk_dec = rdec_ref[2 * CHUNK:2 * CHUNK + 1, :]
    bd_pair = bd[0:PAIR_WIDTH, 0:PAIR_WIDTH]
    intra = rint_ref[...]
    chunks = []
    for c in range(ts // CHUNK):
        rows = slice(c * CHUNK, (c + 1) * CHUNK)
        qc = q_rot[rows]
        kc = k_rot[rows]
        chunks.append(([(qc.astype(BF16), kc.astype(BF16), intra)], rv[rows], qc * q_dec, kc * k_dec, chunk_dec))
    _attention_tile(chunks, ret_state, bd_pair, o_s)
    y_ret = (_head_rmsnorm(o_s[...], rnw_ref[...], bd_bf16) * _silu(rg)).astype(BF16)
    acc = acc + _dot(y_ret, wout_ref[LRU_WIDTH:LRU_WIDTH + GROUP_WIDTH, :])

    ph = _dot(xn, win_ref[:, 2 * LRU_WIDTH + 4 * GROUP_WIDTH:IN_COLS])
    hq = _silu(ph[:, 0:GROUP_WIDTH])
    hf = ph[:, GROUP_WIDTH:2 * GROUP_WIDTH]
    hv = ph[:, 2 * GROUP_WIDTH:3 * GROUP_WIDTH]
    hgate = ph[:, 3 * GROUP_WIDTH:4 * GROUP_WIDTH]
    lbp = lb_ref[...]
    depth = lbp.shape[0]
    mx = lbp[0:1, :]
    for i in range(1, depth):
        mx = jnp.maximum(mx, lbp[i:i + 1, :])
    e = [jnp.exp(lbp[i:i + 1, :] - mx) for i in range(depth)]
    den = functools.reduce(lambda p, q_: p + q_, e)
    lb = jnp.zeros((1, GROUP_WIDTH), F32)
    for i in range(1, layer + 1):
        lb = lb + e[i] / den
    log_sig = jnp.minimum(hf, 0.0) - jnp.log1p(jnp.exp(-jnp.abs(hf)))
    t_a = jnp.log(lb)
    t_b = jnp.log1p(-lb) + log_sig
    t_m = jnp.maximum(t_a, t_b)
    log_f = t_m + jnp.log(jnp.exp(t_a - t_m) + jnp.exp(t_b - t_m))
    hk = (1.0 - lb) * _sigmoid(-hf)

    pfx = pfx_ref[...]
    n_lvl = len(LEVELS)
    sums = []
    for c in range(ts // CHUNK):
        gc = log_f[c * CHUNK:(c + 1) * CHUNK]
        g_hi = gc.astype(BF16)
        g_lo = (gc - g_hi.astype(F32)).astype(BF16)
        sums.append(_dot(pfx, jnp.concatenate([g_hi, g_lo], axis=0)))
    chunks = []
    for c, ps in enumerate(sums):
        rows = slice(c * CHUNK, (c + 1) * CHUNK)
        b_in = ps[0:CHUNK]
        d_last = ps[CHUNK:2 * CHUNK]
        qc = hq[rows]
        kc = hk[rows]
        levels = []
        for li in range(n_lvl):
            decay = jnp.exp(ps[(2 + li) * CHUNK:(3 + li) * CHUNK])
            levels.append(((qc * decay).astype(BF16), (kc * decay).astype(BF16), lvl_ref[li]))
        levels.append((qc.astype(BF16), kc.astype(BF16), lvl_ref[n_lvl]))
        chunks.append((levels, hv[rows], qc * jnp.exp(b_in), kc * jnp.exp(d_last),
                       jnp.exp(b_in[CHUNK - 1:CHUNK, :])))
    _attention_tile(chunks, hg_state, bd_pair, o_s)
    y_hg = (_head_rmsnorm(o_s[...], hnw_ref[...], bd_bf16) * _silu(hgate)).astype(BF16)
    acc = acc + _dot(y_hg, wout_ref[LRU_WIDTH + GROUP_WIDTH:D_MODEL, :])
    out_ref[0] = acc


def _const_spec(shape):
    nd = len(shape)
    return pl.BlockSpec(shape, lambda b, s, _nd=nd: (0,) * _nd, pipeline_mode=pl.Buffered(1))


def _mixer_call(layer, x, params, tables, consts):
    bsz, seq, d = x.shape
    ts = SEQ_TILE
    grid = (bsz, seq // ts)
    tile_spec = pl.BlockSpec((1, ts, d), lambda b, s: (b, s, 0))
    tab_spec = pl.BlockSpec((ts, GROUP_WIDTH), lambda b, s: (s, 0))
    weights = [params[k] for k in ("n1w", "win", "cw", "cb", "wa", "ba", "wx", "bx", "lam",
                                   "rnw", "lb", "hnw", "wout")]
    cvals = [consts[k] for k in ("prefix", "level_masks", "block_diag", "ret_intra", "ret_dec")]
    in_specs = ([tile_spec] + [_const_spec(w.shape) for w in weights] + [tab_spec, tab_spec]
                + [_const_spec(c.shape) for c in cvals])
    scratch = [
        pltpu.VMEM((2 * SUBLANES, LRU_WIDTH), F32),
        pltpu.VMEM((SUBLANES, LRU_WIDTH), F32),
        pltpu.VMEM((SUBLANES, LRU_WIDTH), F32),
        pltpu.VMEM((HEADS // PAIR, PAIR_WIDTH, PAIR_WIDTH), F32),
        pltpu.VMEM((HEADS // PAIR, PAIR_WIDTH, PAIR_WIDTH), F32),
        pltpu.VMEM((ts, GROUP_WIDTH), F32),
    ]
    return pl.pallas_call(
        functools.partial(_mixer_kernel, layer),
        grid=grid,
        in_specs=in_specs,
        out_specs=tile_spec,
        out_shape=jax.ShapeDtypeStruct(x.shape, F32),
        scratch_shapes=scratch,
        compiler_params=pltpu.CompilerParams(
            dimension_semantics=("arbitrary", "arbitrary"),
            vmem_limit_bytes=VMEM_LIMIT_BYTES),
        name=f"mixer_l{layer}",
    )(x, *weights, tables[0], tables[1], *cvals)


def _ffn_kernel(final, x_ref, n2w_ref, wup_ref, cw_ref, cb_ref, wdown_ref, fnw_ref, out_ref,
                ubuf, ucarry, act_s):
    ts = SEQ_TILE
    s_idx = pl.program_id(1)

    @pl.when(s_idx == 0)
    def _():
        ucarry[...] = jnp.zeros_like(ucarry)

    x = x_ref[0]
    h = _rmsnorm(x, n2w_ref[...]).astype(BF16)

    def conv_cols(c0, width):
        u = _dot(h, wup_ref[:, c0:c0 + width])
        cw = [cw_ref[k:k + 1, c0:c0 + width] for k in range(FFN_CONV)]
        y = _causal_conv(u, cw, cb_ref[:, c0:c0 + width], ubuf, ucarry[:, c0:c0 + width])
        ucarry[:, c0:c0 + width] = u[ts - SUBLANES:ts, :]
        return y

    c0 = 0
    while c0 < D_FF:
        width = min(FFN_COLS, D_FF - c0)
        gate = conv_cols(c0, width)
        val = conv_cols(D_FF + c0, width)
        act_s[:, c0:c0 + width] = (_silu(gate) * val).astype(BF16)
        c0 += width
    y = x + _dot(act_s[...], wdown_ref[...])
    if final:
        y = _rmsnorm(y, fnw_ref[...])
    out_ref[0] = y


def _ffn_call(final, x, n2w, wup, cw, cb, wdown, fnw):
    bsz, seq, d = x.shape
    ts = SEQ_TILE
    grid = (bsz, seq // ts)
    tile_spec = pl.BlockSpec((1, ts, d), lambda b, s: (b, s, 0))
    weights = [n2w, wup, cw, cb, wdown, fnw]
    scratch = [
        pltpu.VMEM((2 * SUBLANES, FFN_COLS), F32),
        pltpu.VMEM((SUBLANES, 2 * D_FF), F32),
        pltpu.VMEM((ts, D_FF), BF16),
    ]
    return pl.pallas_call(
        functools.partial(_ffn_kernel, final),
        grid=grid,
        in_specs=[tile_spec] + [_const_spec(w.shape) for w in weights],
        out_specs=tile_spec,
        out_shape=jax.ShapeDtypeStruct(x.shape, F32),
        scratch_shapes=scratch,
        compiler_params=pltpu.CompilerParams(
            dimension_semantics=("arbitrary", "arbitrary"),
            vmem_limit_bytes=VMEM_LIMIT_BYTES),
        name="ffn_final" if final else "ffn",
    )(x, *weights)


def _block_diag(w):
    nb, n, _ = w.shape
    eye = jnp.eye(nb, dtype=w.dtype)
    return (eye[:, None, :, None] * w[:, :, None, :]).reshape(nb * n, nb * n)


def kernel(x, norm1_w, w_in, lru_conv_w, lru_conv_b, lru_wa, lru_ba, lru_wx, lru_bx, lru_lambda,
           ret_norm_w, hg_lower_bounds, hg_norm_w, w_out, norm2_w, ffn_w_up, ffn_conv_w, ffn_conv_b,
           ffn_w_down, final_norm_w):
    bsz, seq, d = x.shape
    depth = w_in.shape[0]
    assert d == D_MODEL and seq % SEQ_TILE == 0
    consts = _constants()
    tables = _rotary_tables(seq)
    row = lambda v: v.reshape(1, -1).astype(F32)
    x = x.astype(F32)
    for l in range(depth):
        params = dict(
            n1w=row(norm1_w[l]), win=w_in[l].astype(BF16), cw=lru_conv_w[l].astype(F32),
            cb=row(lru_conv_b[l]), wa=_block_diag(lru_wa[l]).astype(BF16), ba=row(lru_ba[l]),
            wx=_block_diag(lru_wx[l]).astype(BF16), bx=row(lru_bx[l]), lam=row(lru_lambda[l]),
            rnw=row(ret_norm_w[l]), lb=hg_lower_bounds.astype(F32), hnw=row(hg_norm_w[l]),
            wout=w_out[l].astype(BF16))
        x = _mixer_call(l, x, params, tables, consts)
        x = _ffn_call(l == depth - 1, x, row(norm2_w[l]), ffn_w_up[l].astype(BF16),
                      ffn_conv_w[l].astype(F32), row(ffn_conv_b[l]), ffn_w_down[l].astype(BF16),
                      row(final_norm_w))
    return x
```

```python
import functools

import jax
import jax.numpy as jnp
import numpy as np
from jax import lax
from jax.experimental import pallas as pl
from jax.experimental.pallas import tpu as pltpu

F32 = jnp.float32
BF16 = jnp.bfloat16

D_MODEL = 1024
LRU_WIDTH = 512
LRU_BLOCKS = 8
LRU_CONV = 4
LRU_C = 8.0
HEADS = 4
HEAD_DIM = 64
GROUP_WIDTH = HEADS * HEAD_DIM
ROPE_BASE = 10000.0
IN_COLS = 2 * LRU_WIDTH + 8 * GROUP_WIDTH
D_FF = 2816
FFN_CONV = 3
NORM_EPS = 1e-6

SUBLANES = 8
SEQ_TILE = 512
CHUNK = 64
LEVELS = (1, 2, 4, 8, 16, 32)
PAIR = 2
PAIR_WIDTH = PAIR * HEAD_DIM
FFN_COLS = 512
VMEM_LIMIT_BYTES = 56 * 1024 * 1024


def _dot(a, b):
    return jnp.dot(a, b, preferred_element_type=F32)


def _dot_nt(a, b):
    return lax.dot_general(a, b, (((1,), (1,)), ((), ())), preferred_element_type=F32)


def _dot_tn(a, b):
    return lax.dot_general(a, b, (((0,), (0,)), ((), ())), preferred_element_type=F32)


def _rmsnorm(x, w):
    ms = jnp.mean(x * x, axis=-1, keepdims=True)
    return x * lax.rsqrt(ms + NORM_EPS) * w


def _sigmoid(x):
    return 0.5 * jnp.tanh(0.5 * x) + 0.5


def _silu(x):
    return x * _sigmoid(x)


def _gelu_tanh(x):
    return 0.5 * x * (1.0 + jnp.tanh(np.sqrt(2.0 / np.pi) * (x + 0.044715 * (x * x * x))))


def _split_dot_rhs(x, const_bf16):
    hi = x.astype(BF16)
    lo = (x - hi.astype(F32)).astype(BF16)
    return _dot(hi, const_bf16) + _dot(lo, const_bf16)


def _attention_tile(chunks, state_ref, bd_pair, out_ref):
    c = CHUNK
    pairs = range(HEADS // PAIR)
    lanes = [slice(p * PAIR_WIDTH, (p + 1) * PAIR_WIDTH) for p in pairs]
    low = lax.broadcasted_iota(jnp.int32, (c, PAIR_WIDTH), 1) < HEAD_DIM

    scores = []
    for levels, _, _, _, _ in chunks:
        per_pair = []
        for p in pairs:
            sc = None
            for ql, kl, mask in levels:
                qp = ql[:, lanes[p]]
                zero = jnp.zeros_like(qp)
                lhs = jnp.concatenate([jnp.where(low, qp, zero), jnp.where(low, zero, qp)], axis=0)
                t = (mask if mask.ndim == 2 else mask[p]) * _dot_nt(lhs, kl[:, lanes[p]])
                sc = t if sc is None else sc + t
            per_pair.append(sc.astype(BF16))
        scores.append(per_pair)

    pv, upd = [], []
    for (_, vc, _, k_out, _), per_pair in zip(chunks, scores):
        vps = [vc[:, lanes[p]].astype(BF16) for p in pairs]
        pv.append([_dot(per_pair[p], vps[p]) for p in pairs])
        upd.append([bd_pair * _dot_tn(vps[p], k_out[:, lanes[p]].astype(BF16)) for p in pairs])

    outs = [[None] * len(lanes) for _ in chunks]
    for p in pairs:
        st = state_ref[p]
        for ci, (_, _, q_in, _, dec) in enumerate(chunks):
            o2 = pv[ci][p]
            outs[ci][p] = (jnp.where(low, o2[0:c], o2[c:2 * c])
                           + _dot_nt(q_in[:, lanes[p]].astype(BF16), st.astype(BF16)))
            st = st * dec[:, lanes[p]] + upd[ci][p]
        state_ref[p] = st
    for ci in range(len(chunks)):
        out_ref[ci * c:(ci + 1) * c, :] = jnp.concatenate(outs[ci], axis=1)


def _causal_conv(x, w_rows, b_row, head_buf, tail):
    taps = len(w_rows)
    ts, width = x.shape
    y = b_row + w_rows[taps - 1] * x
    for k in range(taps - 1):
        y = y + w_rows[k] * pltpu.roll(x, taps - 1 - k, 0)
    head_buf[0:SUBLANES, 0:width] = tail
    head_buf[SUBLANES:2 * SUBLANES, 0:width] = x[0:SUBLANES, :]
    yh = b_row + w_rows[taps - 1] * x[0:SUBLANES, :]
    for k in range(taps - 1):
        back = taps - 1 - k
        yh = yh + w_rows[k] * head_buf[SUBLANES - back:2 * SUBLANES - back, 0:width]
    return jnp.concatenate([yh, y[SUBLANES:ts, :]], axis=0)


def _constants():
    c = CHUNK
    n = np.arange(c)[:, None]
    j = np.arange(c)[None, :]
    blocks = [j <= n, j > n]
    masks = []
    n2 = np.arange(PAIR * c)[:, None] % c
    for s in LEVELS:
        start = (n // (2 * s)) * (2 * s)
        anchor = start + s - 1
        second = (n - start) >= s
        blocks.append((second & (j > anchor) & (j <= n)) | ((~second) & (j > n) & (j <= anchor)))
        same = (n2 // (2 * s)) == (j // (2 * s))
        masks.append(same & ((n2 % (2 * s)) >= s) & ((j % (2 * s)) < s))
    masks.append(n2 == j)
    prefix = np.concatenate(blocks, axis=0).astype(np.float32)
    prefix2 = np.concatenate([prefix, prefix], axis=1)
    level_masks = np.stack(masks, axis=0).astype(np.float32)
    r = np.arange(GROUP_WIDTH)
    block_diag = ((r[:, None] // HEAD_DIM) == (r[None, :] // HEAD_DIM)).astype(np.float32)

    log_gamma = np.log1p(-np.exp2(-5.0 - np.arange(HEADS, dtype=np.float64)))
    lg_lane = log_gamma[r // HEAD_DIM][None, :]
    pos = np.arange(c, dtype=np.float64)[:, None]
    rel = (n2 - j).astype(np.float64)
    intra = []
    for p in range(HEADS // PAIR):
        lg_row = log_gamma[PAIR * p + np.arange(PAIR * c) // c][:, None]
        intra.append(np.where(rel >= 0, np.exp(np.maximum(rel, 0.0) * lg_row), 0.0))
    q_dec = np.exp((pos + 1.0) * lg_lane)
    k_dec = np.exp((c - 1.0 - pos) * lg_lane)
    chunk_dec = np.exp(c * lg_lane)
    ret_dec = np.concatenate([q_dec, k_dec, np.broadcast_to(chunk_dec, (SUBLANES, GROUP_WIDTH))], axis=0)
    return dict(
        prefix=jnp.asarray(prefix2, BF16),
        level_masks=jnp.asarray(level_masks, F32),
        block_diag=jnp.asarray(block_diag, F32),
        ret_intra=jnp.asarray(np.stack(intra, axis=0), F32),
        ret_dec=jnp.asarray(ret_dec, F32),
    )


def _rotary_tables(seq):
    inv = ROPE_BASE ** (-jnp.arange(0, HEAD_DIM, 2, dtype=F32) / HEAD_DIM)
    ang = jnp.arange(seq, dtype=F32)[:, None] * inv[None, :]
    cos = jnp.cos(ang)
    sin = jnp.sin(ang)
    cos_t = jnp.tile(cos, (1, 2 * HEADS))
    sin_t = jnp.tile(jnp.concatenate([-sin, sin], axis=1), (1, HEADS))
    return cos_t, sin_t


def _swap_halves(x):
    w = x.shape[-1]
    fwd = pltpu.roll(x, HEAD_DIM // 2, 1)
    bwd = pltpu.roll(x, w - HEAD_DIM // 2, 1)
    lane = lax.broadcasted_iota(jnp.int32, x.shape, 1)
    return jnp.where((lane % HEAD_DIM) < HEAD_DIM // 2, bwd, fwd)


def _head_rmsnorm(o, w, bd_bf16):
    ms = _split_dot_rhs(o * o, bd_bf16) * (1.0 / HEAD_DIM)
    return o * lax.rsqrt(ms + NORM_EPS) * w


def _mixer_kernel(layer,
                  x_ref, n1w_ref, win_ref, cw_ref, cb_ref, wa_ref, ba_ref, wx_ref, bx_ref, lam_ref,
                  rnw_ref, lb_ref, hnw_ref, wout_ref, cos_ref, sin_ref,
                  pfx_ref, lvl_ref, bd_ref, rint_ref, rdec_ref,
                  out_ref,
                  lxbuf, lxtail, hcarry, ret_state, hg_state, o_s):
    ts = SEQ_TILE
    s_idx = pl.program_id(1)

    @pl.when(s_idx == 0)
    def _():
        lxtail[...] = jnp.zeros_like(lxtail)
        hcarry[...] = jnp.zeros_like(hcarry)
        ret_state[...] = jnp.zeros_like(ret_state)
        hg_state[...] = jnp.zeros_like(hg_state)

    x = x_ref[0]
    xn = _rmsnorm(x, n1w_ref[...]).astype(BF16)
    bd = bd_ref[...]
    bd_bf16 = bd.astype(BF16)

    lx = _dot(xn, win_ref[:, 0:LRU_WIDTH])
    lg = _dot(xn, win_ref[:, LRU_WIDTH:2 * LRU_WIDTH])
    cw = [cw_ref[k:k + 1, :] for k in range(LRU_CONV)]
    xc = _causal_conv(lx, cw, cb_ref[...], lxbuf, lxtail[...])
    lxtail[...] = lx[ts - SUBLANES:ts, :]
    xcb = xc.astype(BF16)
    half = LRU_WIDTH // 2
    halves = [slice(0, half), slice(half, LRU_WIDTH)]
    r_gate = _sigmoid(jnp.concatenate([_dot(xcb[:, hs], wa_ref[hs, hs]) for hs in halves], axis=1) + ba_ref[...])
    i_gate = _sigmoid(jnp.concatenate([_dot(xcb[:, hs], wx_ref[hs, hs]) for hs in halves], axis=1) + bx_ref[...])
    z = -lam_ref[...]
    softplus = jnp.maximum(z, 0.0) + jnp.log1p(jnp.exp(-jnp.abs(z)))
    a = jnp.exp(-LRU_C * r_gate * softplus)
    u = jnp.sqrt(1.0 - a * a) * (i_gate * xc)

    pr = _dot(xn, win_ref[:, 2 * LRU_WIDTH:2 * LRU_WIDTH + 4 * GROUP_WIDTH])
    row8 = lax.broadcasted_iota(jnp.int32, (SUBLANES, LRU_WIDTH), 0)
    h_prev = hcarry[0:1, :]
    h_groups = []
    for g in range(ts // SUBLANES):
        rows = slice(g * SUBLANES, (g + 1) * SUBLANES)
        av = a[rows]
        uv = u[rows]
        for d in (1, 2, 4):
            keep = row8 >= d
            a_sh = pltpu.roll(av, d, 0)
            u_sh = pltpu.roll(uv, d, 0)
            uv = jnp.where(keep, av * u_sh + uv, uv)
            av = jnp.where(keep, av * a_sh, av)
        h = av * h_prev + uv
        h_groups.append(h)
        h_prev = h[SUBLANES - 1:SUBLANES, :]
    hcarry[0:1, :] = h_prev
    y_lru = (jnp.concatenate(h_groups, axis=0) * _gelu_tanh(lg)).astype(BF16)

    ph = _dot(xn, win_ref[:, 2 * LRU_WIDTH + 4 * GROUP_WIDTH:IN_COLS])
    cos_t = cos_ref[...]
    sin_t = sin_ref[...]
    rq = pr[:, 0:GROUP_WIDTH]
    rk = pr[:, GROUP_WIDTH:2 * GROUP_WIDTH]
    rv = pr[:, 2 * GROUP_WIDTH:3 * GROUP_WIDTH]
    rg = pr[:, 3 * GROUP_WIDTH:4 * GROUP_WIDTH]
    q_rot = rq * cos_t + _swap_halves(rq) * sin_t
    k_rot = (rk * cos_t + _swap_halves(rk) * sin_t) * (HEAD_DIM ** -0.5)
    q_dec = rdec_ref[0:CHUNK, :]
    k_dec = rdec_ref[CHUNK:2 * CHUNK, :]
    chunk_dec = rdec_ref[2 * CHUNK:2 * CHUNK + 1, :]
    bd_pair = bd[0:PAIR_WIDTH, 0:PAIR_WIDTH]
    intra = rint_ref[...]
    chunks = []
    for c in range(ts // CHUNK):
        rows = slice(c * CHUNK, (c + 1) * CHUNK)
        qc = q_rot[rows]
        kc = k_rot[rows]
        chunks.append(([(qc.astype(BF16), kc.astype(BF16), intra)], rv[rows], qc * q_dec, kc * k_dec, chunk_dec))
    _attention_tile(chunks, ret_state, bd_pair, o_s)
    acc = x + _dot(y_lru, wout_ref[0:LRU_WIDTH, :])
    y_ret = (_head_rmsnorm(o_s[...], rnw_ref[...], bd_bf16) * _silu(rg)).astype(BF16)
    acc = acc + _dot(y_ret, wout_ref[LRU_WIDTH:LRU_WIDTH + GROUP_WIDTH, :])

    hq = _silu(ph[:, 0:GROUP_WIDTH])
    hf = ph[:, GROUP_WIDTH:2 * GROUP_WIDTH]
    hv = ph[:, 2 * GROUP_WIDTH:3 * GROUP_WIDTH]
    hgate = ph[:, 3 * GROUP_WIDTH:4 * GROUP_WIDTH]
    lbp = lb_ref[...]
    depth = lbp.shape[0]
    mx = lbp[0:1, :]
    for i in range(1, depth):
        mx = jnp.maximum(mx, lbp[i:i + 1, :])
    e = [jnp.exp(lbp[i:i + 1, :] - mx) for i in range(depth)]
    den = functools.reduce(lambda p, q_: p + q_, e)
    lb = jnp.zeros((1, GROUP_WIDTH), F32)
    for i in range(1, layer + 1):
        lb = lb + e[i] / den
    log_sig = jnp.minimum(hf, 0.0) - jnp.log1p(jnp.exp(-jnp.abs(hf)))
    t_a = jnp.log(lb)
    t_b = jnp.log1p(-lb) + log_sig
    t_m = jnp.maximum(t_a, t_b)
    log_f = t_m + jnp.log(jnp.exp(t_a - t_m) + jnp.exp(t_b - t_m))
    hk = (1.0 - lb) * _sigmoid(-hf)

    pfx = pfx_ref[...]
    n_lvl = len(LEVELS)
    sums = []
    for c in range(ts // CHUNK):
        gc = log_f[c * CHUNK:(c + 1) * CHUNK]
        g_hi = gc.astype(BF16)
        g_lo = (gc - g_hi.astype(F32)).astype(BF16)
        sums.append(_dot(pfx, jnp.concatenate([g_hi, g_lo], axis=0)))
    chunks = []
    for c, ps in enumerate(sums):
        rows = slice(c * CHUNK, (c + 1) * CHUNK)
        b_in = ps[0:CHUNK]
        d_last = ps[CHUNK:2 * CHUNK]
        qc = hq[rows]
        kc = hk[rows]
        levels = []
        for li in range(n_lvl):
            decay = jnp.exp(ps[(2 + li) * CHUNK:(3 + li) * CHUNK])
            levels.append(((qc * decay).astype(BF16), (kc * decay).astype(BF16), lvl_ref[li]))
        levels.append((qc.astype(BF16), kc.astype(BF16), lvl_ref[n_lvl]))
        chunks.append((levels, hv[rows], qc * jnp.exp(b_in), kc * jnp.exp(d_last),
                       jnp.exp(b_in[CHUNK - 1:CHUNK, :])))
    _attention_tile(chunks, hg_state, bd_pair, o_s)
    y_hg = (_head_rmsnorm(o_s[...], hnw_ref[...], bd_bf16) * _silu(hgate)).astype(BF16)
    acc = acc + _dot(y_hg, wout_ref[LRU_WIDTH + GROUP_WIDTH:D_MODEL, :])
    out_ref[0] = acc


def _const_spec(shape):
    nd = len(shape)
    return pl.BlockSpec(shape, lambda b, s, _nd=nd: (0,) * _nd, pipeline_mode=pl.Buffered(1))


def _mixer_call(layer, x, params, tables, consts):
    bsz, seq, d = x.shape
    ts = SEQ_TILE
    grid = (bsz, seq // ts)
    tile_spec = pl.BlockSpec((1, ts, d), lambda b, s: (b, s, 0))
    tab_spec = pl.BlockSpec((ts, GROUP_WIDTH), lambda b, s: (s, 0))
    weights = [params[k] for k in ("n1w", "win", "cw", "cb", "wa", "ba", "wx", "bx", "lam",
                                   "rnw", "lb", "hnw", "wout")]
    cvals = [consts[k] for k in ("prefix", "level_masks", "block_diag", "ret_intra", "ret_dec")]
    in_specs = ([tile_spec] + [_const_spec(w.shape) for w in weights] + [tab_spec, tab_spec]
                + [_const_spec(c.shape) for c in cvals])
    scratch = [
        pltpu.VMEM((2 * SUBLANES, LRU_WIDTH), F32),
        pltpu.VMEM((SUBLANES, LRU_WIDTH), F32),
        pltpu.VMEM((SUBLANES, LRU_WIDTH), F32),
        pltpu.VMEM((HEADS // PAIR, PAIR_WIDTH, PAIR_WIDTH), F32),
        pltpu.VMEM((HEADS // PAIR, PAIR_WIDTH, PAIR_WIDTH), F32),
        pltpu.VMEM((ts, GROUP_WIDTH), F32),
    ]
    return pl.pallas_call(
        functools.partial(_mixer_kernel, layer),
        grid=grid,
        in_specs=in_specs,
        out_specs=tile_spec,
        out_shape=jax.ShapeDtypeStruct(x.shape, F32),
        scratch_shapes=scratch,
        compiler_params=pltpu.CompilerParams(
            dimension_semantics=("arbitrary", "arbitrary"),
            vmem_limit_bytes=VMEM_LIMIT_BYTES),
        name=f"mixer_l{layer}",
    )(x, *weights, tables[0], tables[1], *cvals)


def _ffn_kernel(final, x_ref, n2w_ref, wup_ref, cw_ref, cb_ref, wdown_ref, fnw_ref, out_ref,
                ubuf, ucarry, act_s):
    ts = SEQ_TILE
    s_idx = pl.program_id(1)

    @pl.when(s_idx == 0)
    def _():
        ucarry[...] = jnp.zeros_like(ucarry)

    x = x_ref[0]
    h = _rmsnorm(x, n2w_ref[...]).astype(BF16)

    def conv_cols(c0, width):
        u = _dot(h, wup_ref[:, c0:c0 + width])
        cw = [cw_ref[k:k + 1, c0:c0 + width] for k in range(FFN_CONV)]
        y = _causal_conv(u, cw, cb_ref[:, c0:c0 + width], ubuf, ucarry[:, c0:c0 + width])
        ucarry[:, c0:c0 + width] = u[ts - SUBLANES:ts, :]
        return y

    c0 = 0
    while c0 < D_FF:
        width = min(FFN_COLS, D_FF - c0)
        gate = conv_cols(c0, width)
        val = conv_cols(D_FF + c0, width)
        act_s[:, c0:c0 + width] = (_silu(gate) * val).astype(BF16)
        c0 += width
    y = x + _dot(act_s[...], wdown_ref[...])
    if final:
        y = _rmsnorm(y, fnw_ref[...])
    out_ref[0] = y


def _ffn_call(final, x, n2w, wup, cw, cb, wdown, fnw):
    bsz, seq, d = x.shape
    ts = SEQ_TILE
    grid = (bsz, seq // ts)
    tile_spec = pl.BlockSpec((1, ts, d), lambda b, s: (b, s, 0))
    weights = [n2w, wup, cw, cb, wdown, fnw]
    scratch = [
        pltpu.VMEM((2 * SUBLANES, FFN_COLS), F32),
        pltpu.VMEM((SUBLANES, 2 * D_FF), F32),
        pltpu.VMEM((ts, D_FF), BF16),
    ]
    return pl.pallas_call(
        functools.partial(_ffn_kernel, final),
        grid=grid,
        in_specs=[tile_spec] + [_const_spec(w.shape) for w in weights],
        out_specs=tile_spec,
        out_shape=jax.ShapeDtypeStruct(x.shape, F32),
        scratch_shapes=scratch,
        compiler_params=pltpu.CompilerParams(
            dimension_semantics=("arbitrary", "arbitrary"),
            vmem_limit_bytes=VMEM_LIMIT_BYTES),
        name="ffn_final" if final else "ffn",
    )(x, *weights)


def _block_diag(w):
    nb, n, _ = w.shape
    eye = jnp.eye(nb, dtype=w.dtype)
    return (eye[:, None, :, None] * w[:, :, None, :]).reshape(nb * n, nb * n)


def kernel(x, norm1_w, w_in, lru_conv_w, lru_conv_b, lru_wa, lru_ba, lru_wx, lru_bx, lru_lambda,
           ret_norm_w, hg_lower_bounds, hg_norm_w, w_out, norm2_w, ffn_w_up, ffn_conv_w, ffn_conv_b,
           ffn_w_down, final_norm_w):
    bsz, seq, d = x.shape
    depth = w_in.shape[0]
    assert d == D_MODEL and seq % SEQ_TILE == 0
    consts = _constants()
    tables = _rotary_tables(seq)
    row = lambda v: v.reshape(1, -1).astype(F32)
    x = x.astype(F32)
    for l in range(depth):
        params = dict(
            n1w=row(norm1_w[l]), win=w_in[l].astype(BF16), cw=lru_conv_w[l].astype(F32),
            cb=row(lru_conv_b[l]), wa=_block_diag(lru_wa[l]).astype(BF16), ba=row(lru_ba[l]),
            wx=_block_diag(lru_wx[l]).astype(BF16), bx=row(lru_bx[l]), lam=row(lru_lambda[l]),
            rnw=row(ret_norm_w[l]), lb=hg_lower_bounds.astype(F32), hnw=row(hg_norm_w[l]),
            wout=w_out[l].astype(BF16))
        x = _mixer_call(l, x, params, tables, consts)
        x = _ffn_call(l == depth - 1, x, row(norm2_w[l]), ffn_w_up[l].astype(BF16),
                      ffn_conv_w[l].astype(F32), row(ffn_conv_b[l]), ffn_w_down[l].astype(BF16),
                      row(final_norm_w))
    return x
```
